```python
import jax
import jax.numpy as jnp
from jax import lax
import numpy as np

D_MODEL = 1024
BATCH = 32
SEQ = 2048
DEPTH = 2

CHUNK = 64
Q_BLOCK = 128
EPS = 1e-6
ROPE_THETA = 10000.0

RET_HEADS = 4
RET_HEAD_DIM = 64
RET_WIDTH = RET_HEADS * RET_HEAD_DIM

MLA_HEADS = 8
MLA_NOPE = 64
MLA_ROPE = 32
MLA_V = 64
MLA_WIDTH = MLA_HEADS * MLA_V
MLA_Q_RANK = 256
MLA_KV_RANK = 128

GLA_HEADS = 4
GLA_DK = 32
GLA_DV = 64
GLA_KWIDTH = GLA_HEADS * GLA_DK
GLA_WIDTH = GLA_HEADS * GLA_DV
GLA_GATE_RANK = 16
GLA_TAU = 16.0

MIX_WIDTH = RET_WIDTH + MLA_WIDTH + GLA_WIDTH

RET_SPLIT = (RET_WIDTH, RET_WIDTH, RET_WIDTH, RET_WIDTH)
MLA_SPLIT = (MLA_Q_RANK, MLA_KV_RANK, MLA_ROPE, MLA_WIDTH)
GLA_SPLIT = (GLA_KWIDTH, GLA_KWIDTH, GLA_WIDTH, GLA_GATE_RANK, GLA_WIDTH)
IN_COLS = sum(RET_SPLIT) + sum(MLA_SPLIT) + sum(GLA_SPLIT)

kernel_name = "hybrid_retention_mla_gla_streaming_block"


def rms_norm(x, w=None):
    xf = x.astype(jnp.float32)
    y = xf * lax.rsqrt(jnp.mean(xf * xf, axis=-1, keepdims=True) + EPS)
    if w is not None:
        y = y * w.astype(jnp.float32)
    return y.astype(x.dtype)


def split_cols(t, sizes):
    cuts = [int(s) for s in np.cumsum(sizes)[:-1]]
    return jnp.split(t, cuts, axis=-1)


def to_heads(t, n_heads):
    b, s, _ = t.shape
    return t.reshape(b, s, n_heads, -1)


def rope(x, pos):
    half = x.shape[-1] // 2
    inv = ROPE_THETA ** (-jnp.arange(half, dtype=jnp.float32) / half)
    ang = pos.astype(jnp.float32)[..., None] * inv
    cos = jnp.cos(ang)[:, :, None, :]
    sin = jnp.sin(ang)[:, :, None, :]
    x1 = x[..., :half].astype(jnp.float32)
    x2 = x[..., half:].astype(jnp.float32)
    out = jnp.concatenate([x1 * cos - x2 * sin, x2 * cos + x1 * sin], axis=-1)
    return out.astype(x.dtype)


def chunk_states(u, a):
    u_t = jnp.moveaxis(u, 1, 0)
    a_t = jnp.moveaxis(a, 1, 0)

    def step(s, inp):
        ui, ai = inp
        return ai[..., None] * s + ui, s

    _, s_prev = lax.scan(step, jnp.zeros_like(u_t[0]), (u_t, a_t))
    return jnp.moveaxis(s_prev, 0, 1)


def retention(q, k, v):
    b, s, h, d = q.shape
    dv = v.shape[-1]
    nc = s // CHUNK
    f32 = jnp.float32
    log_gamma = jnp.log1p(-jnp.exp2(-5.0 - jnp.arange(h, dtype=f32)))
    q = q.astype(f32).reshape(b, nc, CHUNK, h, d)
    k = k.astype(f32).reshape(b, nc, CHUNK, h, d) * (d ** -0.5)
    v = v.astype(f32).reshape(b, nc, CHUNK, h, dv)
    idx = jnp.arange(CHUNK, dtype=f32)
    decay = jnp.exp(log_gamma[:, None, None] * jnp.abs(idx[:, None] - idx[None, :]))
    scores = jnp.einsum('bnihd,bnjhd->bnhij', q, k) * decay
    intra = jnp.einsum('bnhij,bnjhe->bnihe', scores, v)
    k_w = jnp.exp((CHUNK - 1.0 - idx)[:, None] * log_gamma[None, :])
    u = jnp.einsum('bnjhd,jh,bnjhe->bnhde', k, k_w, v)
    a = jnp.broadcast_to(jnp.exp(CHUNK * log_gamma)[:, None], (b, nc, h, d))
    s_prev = chunk_states(u, a)
    q_w = jnp.exp((idx + 1.0)[:, None] * log_gamma[None, :])
    inter = jnp.einsum('bnihd,ih,bnhde->bnihe', q, q_w, s_prev)
    return (intra + inter).reshape(b, s, h, dv)


def mla(q_lat, kv_lat, k_rope, q_norm_w, w_uq, kv_norm_w, w_ukv, pos):
    b, s, _ = q_lat.shape
    q = (rms_norm(q_lat, q_norm_w) @ w_uq).reshape(b, s, MLA_HEADS, MLA_NOPE + MLA_ROPE)
    q_nope = q[..., :MLA_NOPE]
    q_pe = rope(q[..., MLA_NOPE:], pos)
    kv = (rms_norm(kv_lat, kv_norm_w) @ w_ukv).reshape(b, s, MLA_HEADS, MLA_NOPE + MLA_V)
    k_nope = kv[..., :MLA_NOPE]
    v = kv[..., MLA_NOPE:]
    k_pe = rope(k_rope[:, :, None, :], pos)[:, :, 0, :]
    scale = (MLA_NOPE + MLA_ROPE) ** -0.5
    key_chunk = jnp.arange(s) // CHUNK
    nb = s // Q_BLOCK
    qn_b = jnp.moveaxis(q_nope.reshape(b, nb, Q_BLOCK, MLA_HEADS, MLA_NOPE), 1, 0)
    qp_b = jnp.moveaxis(q_pe.reshape(b, nb, Q_BLOCK, MLA_HEADS, MLA_ROPE), 1, 0)
    starts = jnp.arange(nb, dtype=jnp.int32) * Q_BLOCK

    def block(args):
        qn, qp, i0 = args
        sc = (jnp.einsum('bqhd,bkhd->bhqk', qn, k_nope)
              + jnp.einsum('bqhd,bkd->bhqk', qp, k_pe)).astype(jnp.float32) * scale
        q_chunk = (i0 + jnp.arange(Q_BLOCK)) // CHUNK
        mask = key_chunk[None, :] <= q_chunk[:, None]
        sc = jnp.where(mask, sc, -jnp.inf)
        p = jax.nn.softmax(sc, axis=-1).astype(v.dtype)
        return jnp.einsum('bhqk,bkhe->bqhe', p, v)

    out = lax.map(block, (qn_b, qp_b, starts))
    return jnp.moveaxis(out, 0, 1).reshape(b, s, MLA_WIDTH)


def gla(q, k, v, g_low, w_g2, b_g2):
    b, s, h, dk = q.shape
    dv = v.shape[-1]
    nc = s // CHUNK
    f32 = jnp.float32
    log_a = jax.nn.log_sigmoid((g_low @ w_g2 + b_g2).astype(f32)) / GLA_TAU
    cum = jnp.cumsum(log_a.reshape(b, nc, CHUNK, h, dk), axis=2)
    q = q.astype(f32).reshape(b, nc, CHUNK, h, dk)
    k = k.astype(f32).reshape(b, nc, CHUNK, h, dk) * (dk ** -0.5)
    v = v.astype(f32).reshape(b, nc, CHUNK, h, dv)
    e_pos = jnp.exp(cum)
    e_neg = jnp.exp(-cum)
    q_pos = q * e_pos
    past = jnp.einsum('bnihd,bnjhd->bnhij', q_pos, k * e_neg)
    fut = jnp.einsum('bnihd,bnjhd->bnhij', q * e_neg, k * e_pos)
    idx = jnp.arange(CHUNK)
    attn = jnp.where(idx[:, None] >= idx[None, :], past, fut)
    intra = jnp.einsum('bnhij,bnjhe->bnihe', attn, v)
    last = cum[:, :, -1]
    u = jnp.einsum('bnjhd,bnjhe->bnhde', k * jnp.exp(last[:, :, None] - cum), v)
    s_prev = chunk_states(u, jnp.exp(last))
    inter = jnp.einsum('bnihd,bnhde->bnihe', q_pos, s_prev)
    return (intra + inter).reshape(b, s, h, dv)


def setup_inputs(seed: int = 0) -> dict:
    key = jax.random.key(seed)
    ks = jax.random.split(key, 16)
    f32 = jnp.float32

    def normal(k, shape, scale):
        return jax.random.normal(k, shape, f32) * scale

    def gain(k, shape):
        return 1.0 + 0.02 * jax.random.normal(k, shape, f32)

    x = normal(ks[0], (BATCH, SEQ, D_MODEL), 1.0)
    c = normal(ks[1], (BATCH, D_MODEL), 1.0)
    offset = jax.random.randint(ks[2], (BATCH, 1), 0, 4096, dtype=jnp.int32)
    positions = offset + jnp.arange(SEQ, dtype=jnp.int32)[None, :]
    norm_w = gain(ks[3], (DEPTH, D_MODEL))
    ada_w = normal(ks[4], (DEPTH, D_MODEL, 3 * D_MODEL), 0.5 * D_MODEL ** -0.5)
    ada_b = normal(ks[5], (DEPTH, 3 * D_MODEL), 0.02)
    w_in = normal(ks[6], (DEPTH, D_MODEL, IN_COLS), D_MODEL ** -0.5)
    mla_q_norm = gain(ks[7], (DEPTH, MLA_Q_RANK))
    w_uq = normal(ks[8], (DEPTH, MLA_Q_RANK, MLA_HEADS * (MLA_NOPE + MLA_ROPE)), MLA_Q_RANK ** -0.5)
    mla_kv_norm = gain(ks[9], (DEPTH, MLA_KV_RANK))
    w_ukv = normal(ks[10], (DEPTH, MLA_KV_RANK, MLA_HEADS * (MLA_NOPE + MLA_V)), MLA_KV_RANK ** -0.5)
    gla_w_g2 = normal(ks[11], (DEPTH, GLA_GATE_RANK, GLA_KWIDTH), GLA_GATE_RANK ** -0.5)
    gla_b_g2 = normal(ks[12], (DEPTH, GLA_KWIDTH), 0.02)
    gla_norm = gain(ks[13], (DEPTH, GLA_DV))
    w_out = normal(ks[14], (DEPTH, MIX_WIDTH, D_MODEL), MIX_WIDTH ** -0.5)
    final_norm = gain(ks[15], (D_MODEL,))
    return {"x": x, "c": c, "positions": positions, "norm_w": norm_w, "ada_w": ada_w,
            "ada_b": ada_b, "w_in": w_in, "mla_q_norm": mla_q_norm, "w_uq": w_uq,
            "mla_kv_norm": mla_kv_norm, "w_ukv": w_ukv, "gla_w_g2": gla_w_g2,
            "gla_b_g2": gla_b_g2, "gla_norm": gla_norm, "w_out": w_out,
            "final_norm": final_norm}


def reference(x, c, positions, norm_w, ada_w, ada_b, w_in, mla_q_norm, w_uq, mla_kv_norm,
              w_ukv, gla_w_g2, gla_b_g2, gla_norm, w_out, final_norm):
    b, s, _ = x.shape
    c_act = jax.nn.silu(c)
    for l in range(DEPTH):
        shift, scale, gate = jnp.split(c_act @ ada_w[l] + ada_b[l], 3, axis=-1)
        h = rms_norm(x, norm_w[l]) * (1.0 + scale[:, None, :]) + shift[:, None, :]
        proj = h @ w_in[l]
        ret_p, mla_p, gla_p = split_cols(proj, (sum(RET_SPLIT), sum(MLA_SPLIT), sum(GLA_SPLIT)))

        rq, rk, rv, rz = split_cols(ret_p, RET_SPLIT)
        r_o = retention(rope(to_heads(rq, RET_HEADS), positions),
                        rope(to_heads(rk, RET_HEADS), positions),
                        to_heads(rv, RET_HEADS))
        r_o = rms_norm(r_o).astype(x.dtype).reshape(b, s, RET_WIDTH)

        mq, mkv, mkr, mz = split_cols(mla_p, MLA_SPLIT)
        m_o = mla(mq, mkv, mkr, mla_q_norm[l], w_uq[l], mla_kv_norm[l], w_ukv[l], positions)

        gq, gk, gv, gg, gz = split_cols(gla_p, GLA_SPLIT)
        g_o = gla(to_heads(gq, GLA_HEADS), to_heads(gk, GLA_HEADS), to_heads(gv, GLA_HEADS),
                  gg, gla_w_g2[l], gla_b_g2[l])
        g_o = rms_norm(g_o, gla_norm[l]).astype(x.dtype).reshape(b, s, GLA_WIDTH)

        mixed = jnp.concatenate([r_o * jax.nn.silu(rz), m_o * jax.nn.silu(mz),
                                 g_o * jax.nn.silu(gz)], axis=-1)
        x = x + gate[:, None, :] * (mixed @ w_out[l])
    return rms_norm(x, final_norm)
```

```python
import functools

import numpy as np

import jax
import jax.numpy as jnp
from jax import lax
from jax.experimental import pallas as pl
from jax.experimental.pallas import tpu as pltpu

F32 = jnp.float32
BF16 = jnp.bfloat16

D_MODEL = 1024
CHUNK = 64
EPS = 1e-6
ROPE_THETA = 10000.0

RET_HEADS = 4
RET_HEAD_DIM = 64
RET_WIDTH = 256
MLA_HEADS = 8
MLA_NOPE = 64
MLA_ROPE = 32
MLA_V = 64
MLA_WIDTH = 512
MLA_Q_RANK = 256
MLA_KV_RANK = 128
GLA_HEADS = 4
GLA_DK = 32
GLA_DV = 64
GLA_KWIDTH = 128
GLA_WIDTH = 256
GLA_GATE_RANK = 16
GLA_TAU = 16.0
IN_COLS = 2736

LANE = 128

OFF_MZ = 0
OFF_RQ = 512
OFF_RK = 768
OFF_RV = 1024
OFF_RZ = 1280
OFF_MQ = 1536
OFF_GV = 1792
OFF_GZ = 2048
OFF_KV = 2304
OFF_MB = 2432
OFF_MC = 2560
OFF_GQ = 2688
OFF_GK = 2816
NP_COLS = 2944

RET_TILE = 256
GLA_TILE = 256
ATT_TQ = 128
ATT_TK = 256
VMEM_LIMIT = 56 * 1024 * 1024


def _cparams(sem):
    return pltpu.CompilerParams(dimension_semantics=sem, vmem_limit_bytes=VMEM_LIMIT)


def _dot(a, b):
    return jnp.dot(a, b, preferred_element_type=F32)


def _dot_nt(a, b):
    return lax.dot_general(a, b, (((1,), (1,)), ((), ())), preferred_element_type=F32)


def _dot_tn(a, b):
    return lax.dot_general(a, b, (((0,), (0,)), ((), ())), preferred_element_type=F32)


def _silu(x):
    return x / (1.0 + jnp.exp(-x))


def _ada_kernel(c_ref, w_ref, b_ref, o_ref):
    c = c_ref[...]
    o_ref[0] = _dot(_silu(c), w_ref[0]) + b_ref[0]


def _ada_mod(c, ada_w, ada_b):
    depth, d, d3 = ada_w.shape
    b = c.shape[0]
    nblk = d3 // d
    return pl.pallas_call(
        _ada_kernel,
        grid=(depth, nblk),
        in_specs=[
            pl.BlockSpec((b, d), lambda l, j: (0, 0)),
            pl.BlockSpec((1, d, d), lambda l, j: (l, 0, j)),
            pl.BlockSpec((1, 1, d), lambda l, j: (l, 0, j)),
        ],
        out_specs=pl.BlockSpec((1, b, d), lambda l, j: (l, 0, j)),
        out_shape=jax.ShapeDtypeStruct((depth, b, d3), F32),
        compiler_params=_cparams(("arbitrary", "arbitrary")),
        name="ada_mod",
    )(c, ada_w, ada_b.reshape(depth, 1, d3))


def _rope_table_kernel(pos_ref, invr_ref, invm_ref, maskm_ref, cr_ref, sr_ref, cm_ref, sm_ref):
    pos = pos_ref[0].astype(F32)
    ang_r = pos * invr_ref[...]
    cr_ref[0] = jnp.cos(ang_r)
    sr_ref[0] = jnp.sin(ang_r)
    ang_m = pos * invm_ref[...]
    mask = maskm_ref[...]
    cm_ref[0] = jnp.cos(ang_m) * mask
    sm_ref[0] = jnp.sin(ang_m) * mask


def _rope_tables(positions):
    b, s = positions.shape
    t = min(s, 512)
    half_r = RET_HEAD_DIM // 2
    half_m = MLA_ROPE // 2
    inv_r = ROPE_THETA ** (-jnp.arange(half_r, dtype=F32) / half_r)
    inv_m = ROPE_THETA ** (-jnp.arange(half_m, dtype=F32) / half_m)
    invr_row = jnp.tile(inv_r, LANE // half_r).reshape(1, LANE)
    invm_row = jnp.tile(inv_m, LANE // half_m).reshape(1, LANE)
    maskm_row = (jnp.arange(LANE) < MLA_ROPE).astype(F32).reshape(1, LANE)
    row = pl.BlockSpec((1, LANE), lambda i, j: (0, 0))
    tab = pl.BlockSpec((1, t, LANE), lambda i, j: (i, j, 0))
    shp = jax.ShapeDtypeStruct((b, s, LANE), F32)
    return pl.pallas_call(
        _rope_table_kernel,
        grid=(b, s // t),
        in_specs=[pl.BlockSpec((1, t, 1), lambda i, j: (i, j, 0)), row, row, row],
        out_specs=[tab, tab, tab, tab],
        out_shape=[shp, shp, shp, shp],
        compiler_params=_cparams(("arbitrary", "arbitrary")),
        name="rope_tables",
    )(positions.reshape(b, s, 1), invr_row, invm_row, maskm_row)


IN_NCHUNK = 512


def _in_proj_kernel(x_ref, mod_ref, nw_ref, w_ref, o_ref):
    x = x_ref[0]
    shift = mod_ref[0, 0:1, :]
    scale = mod_ref[0, 1:2, :]
    y = x * lax.rsqrt(jnp.mean(x * x, axis=-1, keepdims=True) + EPS)
    y = y * nw_ref[...]
    h = (y * (1.0 + scale) + shift).astype(BF16)
    ncols = o_ref.shape[-1]
    for c0 in range(0, ncols, IN_NCHUNK):
        c1 = min(c0 + IN_NCHUNK, ncols)
        o_ref[0, :, c0:c1] = _dot(h, w_ref[:, c0:c1]).astype(BF16)


def _in_proj(x, mod3, norm_w_row, w_packed, t):
    b, s, d = x.shape
    ncols = w_packed.shape[1]
    return pl.pallas_call(
        _in_proj_kernel,
        grid=(b, s // t),
        in_specs=[
            pl.BlockSpec((1, t, d), lambda i, j: (i, j, 0)),
            pl.BlockSpec((1, 3, d), lambda i, j: (i, 0, 0)),
            pl.BlockSpec((1, d), lambda i, j: (0, 0)),
            pl.BlockSpec((d, ncols), lambda i, j: (0, 0)),
        ],
        out_specs=pl.BlockSpec((1, t, ncols), lambda i, j: (i, j, 0)),
        out_shape=jax.ShapeDtypeStruct((b, s, ncols), BF16),
        compiler_params=_cparams(("arbitrary", "arbitrary")),
        name="in_proj",
    )(x, mod3, norm_w_row, w_packed)


def _retention_kernel(q_ref, k_ref, v_ref, z_ref, cos_ref, sin_ref, dmat_ref, qw_ref, kw_ref,
                      sdec_ref, bd_ref, ones_ref, o_ref, state_ref):
    t = q_ref.shape[1]

    @pl.when(pl.program_id(1) == 0)
    def _():
        state_ref[...] = jnp.zeros_like(state_ref)

    cos = cos_ref[0]
    sin = sin_ref[0]

    def rope(ref):
        a = ref[0].astype(F32)
        x1 = a[:, :LANE]
        x2 = a[:, LANE:]
        return jnp.concatenate([x1 * cos - x2 * sin, x2 * cos + x1 * sin], axis=-1)

    q = rope(q_ref)
    k = rope(k_ref) * (RET_HEAD_DIM ** -0.5)
    v = v_ref[0]

    lane = lax.broadcasted_iota(jnp.int32, (1, RET_WIDTH), 1)
    qk_head = (lane % LANE) // (RET_HEAD_DIM // 2)
    v_head = lane // RET_HEAD_DIM
    zero_bf = jnp.zeros((), BF16)
    k_bf = k.astype(BF16)
    k_stack = jnp.concatenate(
        [jnp.where(qk_head == h, k_bf, zero_bf) for h in range(RET_HEADS)], axis=0)
    v_stack = jnp.concatenate(
        [jnp.where(v_head == h, v, zero_bf) for h in range(RET_HEADS)], axis=0)

    scores = _dot_nt(q.astype(BF16), k_stack) * dmat_ref[...]
    intra = _dot(scores.astype(BF16), v_stack)

    state = state_ref[...]
    inter = _dot((q * qw_ref[...]).astype(BF16), state.astype(BF16))
    u = _dot_tn((k * kw_ref[...]).astype(BF16), v) * bd_ref[...]
    state_ref[...] = state * sdec_ref[...] + u

    o = intra + inter
    ms = _dot((o * o).astype(BF16), ones_ref[...])
    y = o * lax.rsqrt(ms + EPS)
    z = z_ref[0].astype(F32)
    o_ref[0] = (y * _silu(z)).astype(BF16)


def _retention_consts(t):
    f32 = F32
    h = RET_HEADS
    log_gamma = jnp.log1p(-jnp.exp2(-5.0 - jnp.arange(h, dtype=f32)))
    idx = jnp.arange(t, dtype=f32)
    dist = jnp.abs(idx[:, None] - idx[None, :])
    ci = jnp.arange(t) // CHUNK
    vis = (ci[None, :] <= ci[:, None]).astype(f32)
    dmat = jnp.exp(log_gamma[:, None, None] * dist[None]) * vis[None]
    dmat = jnp.moveaxis(dmat, 0, 1).reshape(t, h * t)
    lane = jnp.arange(RET_WIDTH)
    qk_head = (lane % LANE) // (RET_HEAD_DIM // 2)
    v_head = lane // RET_HEAD_DIM
    lg_lane = log_gamma[qk_head]
    qw = jnp.exp((idx + 1.0)[:, None] * lg_lane[None, :])
    kw = jnp.exp((t - 1.0 - idx)[:, None] * lg_lane[None, :])
    bd = (qk_head[:, None] == v_head[None, :]).astype(f32)
    sdec = jnp.exp(t * lg_lane)[:, None] * bd
    ones = ((v_head[:, None] == v_head[None, :]).astype(f32) / RET_HEAD_DIM).astype(BF16)
    return dmat, qw, kw, sdec, bd, ones


def _retention(proj, cos_r, sin_r):
    b, s, _ = proj.shape
    t = min(RET_TILE, s)
    dmat, qw, kw, sdec, bd, ones = _retention_consts(t)
    w = RET_WIDTH

    def col(off):
        return pl.BlockSpec((1, t, w), lambda i, j, o=off // w: (i, j, o))

    def const(a):
        return pl.BlockSpec(a.shape, lambda i, j: (0, 0))

    tab = pl.BlockSpec((1, t, LANE), lambda i, j: (i, j, 0))
    return pl.pallas_call(
        _retention_kernel,
        grid=(b, s // t),
        in_specs=[col(OFF_RQ), col(OFF_RK), col(OFF_RV), col(OFF_RZ), tab, tab,
                  const(dmat), const(qw), const(kw), const(sdec), const(bd), const(ones)],
        out_specs=pl.BlockSpec((1, t, w), lambda i, j: (i, j, 0)),
        out_shape=jax.ShapeDtypeStruct((b, s, w), BF16),
        scratch_shapes=[pltpu.VMEM((w, w), F32)],
        compiler_params=_cparams(("arbitrary", "arbitrary")),
        name="retention",
    )(proj, proj, proj, proj, cos_r, sin_r, dmat, qw, kw, sdec, bd, ones)


def _gla_kernel(q_ref, k_ref, v_ref, z_ref, mb_ref, wg_ref, bg_ref, ltri_ref, bdt_ref, ones_ref,
                nw_ref, o_ref, state_ref):
    t = q_ref.shape[1]
    nchunk = t // CHUNK

    @pl.when(pl.program_id(1) == 0)
    def _():
        state_ref[...] = jnp.zeros_like(state_ref)

    g = _dot(mb_ref[0], wg_ref[...]) + bg_ref[...]
    log_a = (jnp.minimum(g, 0.0) - jnp.log1p(jnp.exp(-jnp.abs(g)))) / GLA_TAU
    la_hi = log_a.astype(BF16)
    la_lo = (log_a - la_hi.astype(F32)).astype(BF16)
    ltri = ltri_ref[...]
    cum = _dot(ltri, la_hi) + _dot(ltri, la_lo)

    q = q_ref[0].astype(F32)
    k = k_ref[0].astype(F32) * (GLA_DK ** -0.5)
    e_pos = jnp.exp(cum)
    e_neg = jnp.exp(-cum)
    q_pos = (q * e_pos).astype(BF16)
    q_neg = (q * e_neg).astype(BF16)
    k_neg = (k * e_neg).astype(BF16)
    k_pos = (k * e_pos).astype(BF16)

    lane_k = lax.broadcasted_iota(jnp.int32, (1, GLA_KWIDTH), 1) // GLA_DK
    lane_v = lax.broadcasted_iota(jnp.int32, (1, GLA_WIDTH), 1) // GLA_DV
    row_i = lax.broadcasted_iota(jnp.int32, (CHUNK, GLA_HEADS * CHUNK), 0)
    col_j = lax.broadcasted_iota(jnp.int32, (CHUNK, GLA_HEADS * CHUNK), 1) % CHUNK
    causal = row_i >= col_j
    zero_bf = jnp.zeros((), BF16)
    bdt = bdt_ref[...]
    ones = ones_ref[...]
    nw = nw_ref[...]

    def stack(a, head_of_lane):
        return jnp.concatenate(
            [jnp.where(head_of_lane == h, a, zero_bf) for h in range(GLA_HEADS)], axis=0)

    state_t = state_ref[...]
    for c in range(nchunk):
        r0, r1 = c * CHUNK, (c + 1) * CHUNK
        v_c = v_ref[0, r0:r1, :]
        past = _dot_nt(q_pos[r0:r1], stack(k_neg[r0:r1], lane_k))
        fut = _dot_nt(q_neg[r0:r1], stack(k_pos[r0:r1], lane_k))
        attn = jnp.where(causal, past, fut).astype(BF16)
        intra = _dot(attn, stack(v_c, lane_v))
        inter = _dot_nt(q_pos[r0:r1], state_t.astype(BF16))
        last = cum[r1 - 1:r1, :]
        k_st = (k[r0:r1] * jnp.exp(last - cum[r0:r1])).astype(BF16)
        u_t = _dot_tn(v_c, k_st) * bdt
        state_t = state_t * jnp.exp(last) + u_t
        o = intra + inter
        ms = _dot((o * o).astype(BF16), ones)
        y = o * lax.rsqrt(ms + EPS) * nw
        z = z_ref[0, r0:r1, :].astype(F32)
        o_ref[0, r0:r1, :] = (y * _silu(z)).astype(BF16)
    state_ref[...] = state_t


def _gla(proj, w_g2p, b_g2_row, gla_norm_row):
    b, s, _ = proj.shape
    t = min(GLA_TILE, s)
    idx = jnp.arange(t)
    ltri = ((idx[:, None] >= idx[None, :]) & (idx[:, None] // CHUNK == idx[None, :] // CHUNK)).astype(BF16)
    kh = jnp.arange(GLA_KWIDTH) // GLA_DK
    vh = jnp.arange(GLA_WIDTH) // GLA_DV
    bdt = (vh[:, None] == kh[None, :]).astype(F32)
    ones = ((vh[:, None] == vh[None, :]).astype(F32) / GLA_DV).astype(BF16)

    def col(off, w):
        return pl.BlockSpec((1, t, w), lambda i, j, o=off // w: (i, j, o))

    def const(a):
        return pl.BlockSpec(a.shape, lambda i, j: (0, 0))

    return pl.pallas_call(
        _gla_kernel,
        grid=(b, s // t),
        in_specs=[col(OFF_GQ, GLA_KWIDTH), col(OFF_GK, GLA_KWIDTH), col(OFF_GV, GLA_WIDTH),
                  col(OFF_GZ, GLA_WIDTH), col(OFF_MB, LANE),
                  const(w_g2p), const(b_g2_row), const(ltri), const(bdt), const(ones),
                  const(gla_norm_row)],
        out_specs=pl.BlockSpec((1, t, GLA_WIDTH), lambda i, j: (i, j, 0)),
        out_shape=jax.ShapeDtypeStruct((b, s, GLA_WIDTH), BF16),
        scratch_shapes=[pltpu.VMEM((GLA_WIDTH, GLA_KWIDTH), F32)],
        compiler_params=_cparams(("arbitrary", "arbitrary")),
        name="gla",
    )(proj, proj, proj, proj, proj, w_g2p, b_g2_row, ltri, bdt, ones, gla_norm_row)


def _absorb_kernel(wq_ref, wk_ref, o_ref):
    o_ref[0, 0] = lax.dot_general(wq_ref[0, 0], wk_ref[0, 0], (((1,), (1,)), ((), ())),
                                  precision=lax.Precision.HIGHEST, preferred_element_type=F32)


def _absorb_weights(wq_nope, wk_nope):
    depth, h, r, dn = wq_nope.shape
    kr = wk_nope.shape[2]
    return pl.pallas_call(
        _absorb_kernel,
        grid=(depth, h),
        in_specs=[pl.BlockSpec((1, 1, r, dn), lambda l, i: (l, i, 0, 0)),
                  pl.BlockSpec((1, 1, kr, dn), lambda l, i: (l, i, 0, 0))],
        out_specs=pl.BlockSpec((1, 1, r, kr), lambda l, i: (l, i, 0, 0)),
        out_shape=jax.ShapeDtypeStruct((depth, h, r, kr), F32),
        compiler_params=_cparams(("arbitrary", "arbitrary")),
        name="mla_absorb",
    )(wq_nope, wk_nope)


def _mla_prep_kernel(mq_ref, kv_ref, mb_ref, mc_ref, cos_ref, sin_ref, qnw_ref, kvnw_ref,
                     wq_ref, wrot_ref, q_out, k_out, vt_out):
    cos = cos_ref[0]
    sin = sin_ref[0]
    lat = mq_ref[0].astype(F32)
    lat = lat * lax.rsqrt(jnp.mean(lat * lat, axis=-1, keepdims=True) + EPS) * qnw_ref[...]
    lat = lat.astype(BF16)
    scale = (MLA_NOPE + MLA_ROPE) ** -0.5
    for h in range(MLA_HEADS):
        main = _dot(lat, wq_ref[:, 2 * LANE * h:2 * LANE * (h + 1)])
        rot = _dot(lat, wrot_ref[:, LANE * h:LANE * (h + 1)])
        q_out[0, h, :, :LANE] = (main[:, :LANE] * scale).astype(BF16)
        q_out[0, h, :, LANE:] = ((main[:, LANE:] * cos + rot * sin) * scale).astype(BF16)
    kv = kv_ref[0].astype(F32)
    kvn = kv * lax.rsqrt(jnp.mean(kv * kv, axis=-1, keepdims=True) + EPS) * kvnw_ref[...]
    kvn_bf = kvn.astype(BF16)
    k_out[0, :, :LANE] = kvn_bf
    k_out[0, :, LANE:] = (mb_ref[0].astype(F32) * cos + mc_ref[0].astype(F32) * sin).astype(BF16)
    vt_out[0] = kvn.T.astype(BF16)


def _mla_prep(proj, cos_m, sin_m, qn_row, kvn_row, wq_all, wrot):
    b, s, _ = proj.shape
    t = min(512, s)

    def col(off, w):
        return pl.BlockSpec((1, t, w), lambda i, j, o=off // w: (i, j, o))

    def const(a):
        return pl.BlockSpec(a.shape, lambda i, j: (0, 0))

    tab = pl.BlockSpec((1, t, LANE), lambda i, j: (i, j, 0))
    return pl.pallas_call(
        _mla_prep_kernel,
        grid=(b, s // t),
        in_specs=[col(OFF_MQ, MLA_Q_RANK), col(OFF_KV, LANE), col(OFF_MB, LANE), col(OFF_MC, LANE),
                  tab, tab, const(qn_row), const(kvn_row), const(wq_all), const(wrot)],
        out_specs=[pl.BlockSpec((1, MLA_HEADS, t, 2 * LANE), lambda i, j: (i, 0, j, 0)),
                   pl.BlockSpec((1, t, 2 * LANE), lambda i, j: (i, j, 0)),
                   pl.BlockSpec((1, LANE, t), lambda i, j: (i, 0, j))],
        out_shape=[jax.ShapeDtypeStruct((b, MLA_HEADS, s, 2 * LANE), BF16),
                   jax.ShapeDtypeStruct((b, s, 2 * LANE), BF16),
                   jax.ShapeDtypeStruct((b, LANE, s), BF16)],
        compiler_params=_cparams(("arbitrary", "arbitrary")),
        name="mla_prep",
    )(proj, proj, proj, proj, cos_m, sin_m, qn_row, kvn_row, wq_all, wrot)


def _mla_attn_kernel(q_ref, k_ref, vt_ref, z_ref, wuv_ref, o_ref, m_ref, l_ref, acc_ref):
    tq = q_ref.shape[2]
    nq = MLA_HEADS * tq
    tk = ATT_TK
    qi = pl.program_id(1)
    q = q_ref[0].reshape(nq, 2 * LANE)

    m_ref[...] = jnp.full_like(m_ref, -jnp.inf)
    l_ref[...] = jnp.zeros_like(l_ref)
    acc_ref[...] = jnp.zeros_like(acc_ref)

    def step(kt, masked):
        k0 = pl.multiple_of(kt * tk, tk)
        k = k_ref[0, pl.ds(k0, tk), :]
        vt = vt_ref[0, :, pl.ds(k0, tk)]
        s = _dot_nt(k, q)
        if masked:
            key_chunk = (k0 + lax.broadcasted_iota(jnp.int32, (tk, nq), 0)) // CHUNK
            q_chunk = (qi * tq + lax.broadcasted_iota(jnp.int32, (tk, nq), 1) % tq) // CHUNK
            s = jnp.where(key_chunk <= q_chunk, s, -jnp.inf)
        m_prev = m_ref[...]
        m_new = jnp.maximum(m_prev, jnp.max(s, axis=0, keepdims=True))
        alpha = jnp.exp(m_prev - m_new)
        p = jnp.exp(s - m_new)
        l_ref[...] = alpha * l_ref[...] + jnp.sum(p, axis=0, keepdims=True)
        acc_ref[...] = alpha * acc_ref[...] + _dot(vt, p.astype(BF16))
        m_ref[...] = m_new

    n_full = (qi * tq) // tk

    def body(kt, carry):
        step(kt, False)
        return carry

    lax.fori_loop(0, n_full, body, 0)
    step(n_full, True)

    o_lat = acc_ref[...] / l_ref[...]
    outs = []
    for h in range(MLA_HEADS):
        o_h = o_lat[:, h * tq:(h + 1) * tq].astype(BF16)
        outs.append(_dot(wuv_ref[h], o_h))
    o_t = jnp.concatenate(outs, axis=0)
    z = z_ref[0].astype(F32)
    o_ref[0] = (o_t.T * _silu(z)).astype(BF16)


def _mla_attn(q_s, k_c, v_t, proj, wuv):
    b, h, s, _ = q_s.shape
    tq = min(ATT_TQ, s)
    nq = h * tq
    return pl.pallas_call(
        _mla_attn_kernel,
        grid=(b, s // tq),
        in_specs=[pl.BlockSpec((1, h, tq, 2 * LANE), lambda i, j: (i, 0, j, 0)),
                  pl.BlockSpec((1, s, 2 * LANE), lambda i, j: (i, 0, 0)),
                  pl.BlockSpec((1, LANE, s), lambda i, j: (i, 0, 0)),
                  pl.BlockSpec((1, tq, MLA_WIDTH), lambda i, j: (i, j, OFF_MZ // MLA_WIDTH)),
                  pl.BlockSpec(wuv.shape, lambda i, j: (0, 0, 0))],
        out_specs=pl.BlockSpec((1, tq, MLA_WIDTH), lambda i, j: (i, j, 0)),
        out_shape=jax.ShapeDtypeStruct((b, s, MLA_WIDTH), BF16),
        scratch_shapes=[pltpu.VMEM((1, nq), F32), pltpu.VMEM((1, nq), F32),
                        pltpu.VMEM((LANE, nq), F32)],
        compiler_params=_cparams(("arbitrary", "arbitrary")),
        name="mla_attn",
    )(q_s, k_c, v_t, proj, wuv)


def _out_proj_kernel(r_ref, m_ref, g_ref, x_ref, mod_ref, w_ref, fw_ref, o_ref, *, final):
    y = _dot(r_ref[0], w_ref[0:RET_WIDTH, :])
    y += _dot(m_ref[0], w_ref[RET_WIDTH:RET_WIDTH + MLA_WIDTH, :])
    y += _dot(g_ref[0], w_ref[RET_WIDTH + MLA_WIDTH:, :])
    gate = mod_ref[0, 2:3, :]
    x = x_ref[0] + gate * y
    if final:
        x = x * lax.rsqrt(jnp.mean(x * x, axis=-1, keepdims=True) + EPS) * fw_ref[...]
    o_ref[0] = x


def _out_proj(r_o, m_o, g_o, x, mod3, w_out_bf, final_row, final, t):
    b, s, d = x.shape

    def tok(w):
        return pl.BlockSpec((1, t, w), lambda i, j: (i, j, 0))

    return pl.pallas_call(
        functools.partial(_out_proj_kernel, final=final),
        grid=(b, s // t),
        in_specs=[tok(RET_WIDTH), tok(MLA_WIDTH), tok(GLA_WIDTH), tok(d),
                  pl.BlockSpec((1, 3, d), lambda i, j: (i, 0, 0)),
                  pl.BlockSpec(w_out_bf.shape, lambda i, j: (0, 0)),
                  pl.BlockSpec((1, d), lambda i, j: (0, 0))],
        out_specs=tok(d),
        out_shape=jax.ShapeDtypeStruct((b, s, d), F32),
        compiler_params=_cparams(("arbitrary", "arbitrary")),
        name="out_proj",
    )(r_o, m_o, g_o, x, mod3, w_out_bf, final_row)


def _pack_w_in(w_in_l):
    src = np.full((NP_COLS,), -1, np.int64)
    sign = np.ones((NP_COLS,), np.float32)
    ret0, mla0, gla0 = 0, 4 * RET_WIDTH, 4 * RET_WIDTH + (MLA_Q_RANK + MLA_KV_RANK + MLA_ROPE + MLA_WIDTH)
    half = RET_HEAD_DIM // 2
    for p in range(LANE):
        hh, i = p // half, p % half
        for base, off in ((ret0, OFF_RQ), (ret0 + RET_WIDTH, OFF_RK)):
            src[off + p] = base + hh * RET_HEAD_DIM + i
            src[off + LANE + p] = base + hh * RET_HEAD_DIM + half + i
    src[OFF_RV:OFF_RV + RET_WIDTH] = ret0 + 2 * RET_WIDTH + np.arange(RET_WIDTH)
    src[OFF_RZ:OFF_RZ + RET_WIDTH] = ret0 + 3 * RET_WIDTH + np.arange(RET_WIDTH)
    src[OFF_MQ:OFF_MQ + MLA_Q_RANK] = mla0 + np.arange(MLA_Q_RANK)
    src[OFF_KV:OFF_KV + MLA_KV_RANK] = mla0 + MLA_Q_RANK + np.arange(MLA_KV_RANK)
    kr0 = mla0 + MLA_Q_RANK + MLA_KV_RANK
    src[OFF_MB:OFF_MB + MLA_ROPE] = kr0 + np.arange(MLA_ROPE)
    hm = MLA_ROPE // 2
    src[OFF_MC:OFF_MC + hm] = kr0 + hm + np.arange(hm)
    sign[OFF_MC:OFF_MC + hm] = -1.0
    src[OFF_MC + hm:OFF_MC + MLA_ROPE] = kr0 + np.arange(hm)
    src[OFF_MZ:OFF_MZ + MLA_WIDTH] = kr0 + MLA_ROPE + np.arange(MLA_WIDTH)
    src[OFF_GQ:OFF_GQ + GLA_KWIDTH] = gla0 + np.arange(GLA_KWIDTH)
    src[OFF_GK:OFF_GK + GLA_KWIDTH] = gla0 + GLA_KWIDTH + np.arange(GLA_KWIDTH)
    src[OFF_GV:OFF_GV + GLA_WIDTH] = gla0 + 2 * GLA_KWIDTH + np.arange(GLA_WIDTH)
    gg0 = gla0 + 2 * GLA_KWIDTH + GLA_WIDTH
    src[OFF_MB + MLA_ROPE:OFF_MB + MLA_ROPE + GLA_GATE_RANK] = gg0 + np.arange(GLA_GATE_RANK)
    src[OFF_GZ:OFF_GZ + GLA_WIDTH] = gg0 + GLA_GATE_RANK + np.arange(GLA_WIDTH)
    valid = src >= 0
    gathered = jnp.take(w_in_l, jnp.asarray(np.where(valid, src, 0)), axis=1)
    packed = gathered * jnp.asarray(np.where(valid, sign, 0.0))[None, :]
    return packed.astype(BF16)


def _pack_mla_q(w_uq_l, w_abs_l):
    r = w_uq_l.shape[0]
    hd = MLA_NOPE + MLA_ROPE
    hm = MLA_ROPE // 2
    w3 = w_uq_l.reshape(r, MLA_HEADS, hd)
    pe = w3[:, :, MLA_NOPE:]
    pad = jnp.zeros((r, MLA_HEADS, LANE - MLA_ROPE), F32)
    main = jnp.concatenate([jnp.moveaxis(w_abs_l, 0, 1), pe, pad], axis=-1)
    rot = jnp.concatenate([-pe[:, :, hm:], pe[:, :, :hm], pad], axis=-1)
    return (main.reshape(r, MLA_HEADS * 2 * LANE).astype(BF16),
            rot.reshape(r, MLA_HEADS * LANE).astype(BF16))


def kernel(x, c, positions, norm_w, ada_w, ada_b, w_in, mla_q_norm, w_uq, mla_kv_norm, w_ukv,
           gla_w_g2, gla_b_g2, gla_norm, w_out, final_norm):
    b, s, d = x.shape
    depth = w_in.shape[0]
    t_tok = min(512, s)

    mod = _ada_mod(c, ada_w, ada_b).reshape(depth, b, 3, d)
    cos_r, sin_r, cos_m, sin_m = _rope_tables(positions)

    kv_hd = MLA_NOPE + MLA_V
    q_hd = MLA_NOPE + MLA_ROPE
    w_ukv4 = w_ukv.reshape(depth, MLA_KV_RANK, MLA_HEADS, kv_hd)
    wk_nope = jnp.moveaxis(w_ukv4[..., :MLA_NOPE], 2, 1)
    wuv = jnp.transpose(w_ukv4[..., MLA_NOPE:], (0, 2, 3, 1)).astype(BF16)
    wq_nope = jnp.moveaxis(
        w_uq.reshape(depth, MLA_Q_RANK, MLA_HEADS, q_hd)[..., :MLA_NOPE], 2, 1)
    w_abs = _absorb_weights(wq_nope, wk_nope)

    final_row = final_norm.reshape(1, d)
    for l in range(depth):
        w_packed = _pack_w_in(w_in[l])
        wq_all, wrot = _pack_mla_q(w_uq[l], w_abs[l])
        w_g2p = jnp.zeros((LANE, GLA_KWIDTH), F32).at[MLA_ROPE:MLA_ROPE + GLA_GATE_RANK].set(
            gla_w_g2[l]).astype(BF16)

        proj = _in_proj(x, mod[l], norm_w[l].reshape(1, d), w_packed, t_tok)
        r_o = _retention(proj, cos_r, sin_r)
        g_o = _gla(proj, w_g2p, gla_b_g2[l].reshape(1, GLA_KWIDTH),
                   jnp.tile(gla_norm[l], GLA_HEADS).reshape(1, GLA_WIDTH))
        q_s, k_c, v_t = _mla_prep(proj, cos_m, sin_m, mla_q_norm[l].reshape(1, MLA_Q_RANK),
                                  mla_kv_norm[l].reshape(1, MLA_KV_RANK), wq_all, wrot)
        m_o = _mla_attn(q_s, k_c, v_t, proj, wuv[l])
        x = _out_proj(r_o, m_o, g_o, x, mod[l], w_out[l].astype(BF16), final_row,
                      l == depth - 1, t_tok)
    return x
```

```python
import functools

import numpy as np

import jax
import jax.numpy as jnp
from jax import lax
from jax.experimental import pallas as pl
from jax.experimental.pallas import tpu as pltpu

F32 = jnp.float32
BF16 = jnp.bfloat16

D_MODEL = 1024
CHUNK = 64
EPS = 1e-6
ROPE_THETA = 10000.0

RET_HEADS = 4
RET_HEAD_DIM = 64
RET_WIDTH = 256
MLA_HEADS = 8
MLA_NOPE = 64
MLA_ROPE = 32
MLA_V = 64
MLA_WIDTH = 512
MLA_Q_RANK = 256
MLA_KV_RANK = 128
GLA_HEADS = 4
GLA_DK = 32
GLA_DV = 64
GLA_KWIDTH = 128
GLA_WIDTH = 256
GLA_GATE_RANK = 16
GLA_TAU = 16.0
IN_COLS = 2736

LANE = 128

OFF_MZ = 0
OFF_RQ = 512
OFF_RK = 768
OFF_RV = 1024
OFF_RZ = 1280
OFF_MQ = 1536
OFF_GV = 1792
OFF_GZ = 2048
OFF_KV = 2304
OFF_MB = 2432
OFF_MC = 2560
OFF_GQ = 2688
OFF_GK = 2816
NP_COLS = 2944

RET_TILE = 256
GLA_TILE = 256
ATT_T = 256
VT_ROWS = 144
LOG2E = 1.4426950408889634
MASK_NEG = -1e30
VMEM_LIMIT = 56 * 1024 * 1024


def _cparams(sem):
    return pltpu.CompilerParams(dimension_semantics=sem, vmem_limit_bytes=VMEM_LIMIT)


def _dot(a, b):
    return jnp.dot(a, b, preferred_element_type=F32)


def _dot_nt(a, b):
    return lax.dot_general(a, b, (((1,), (1,)), ((), ())), preferred_element_type=F32)


def _dot_tn(a, b):
    return lax.dot_general(a, b, (((0,), (0,)), ((), ())), preferred_element_type=F32)


def _silu(x):
    return x / (1.0 + jnp.exp(-x))


def _ada_kernel(c_ref, w_ref, b_ref, o_ref):
    c = c_ref[...]
    o_ref[0] = _dot(_silu(c), w_ref[0]) + b_ref[0]


def _ada_mod(c, ada_w, ada_b):
    depth, d, d3 = ada_w.shape
    b = c.shape[0]
    nblk = d3 // d
    return pl.pallas_call(
        _ada_kernel,
        grid=(depth, nblk),
        in_specs=[
            pl.BlockSpec((b, d), lambda l, j: (0, 0)),
            pl.BlockSpec((1, d, d), lambda l, j: (l, 0, j)),
            pl.BlockSpec((1, 1, d), lambda l, j: (l, 0, j)),
        ],
        out_specs=pl.BlockSpec((1, b, d), lambda l, j: (l, 0, j)),
        out_shape=jax.ShapeDtypeStruct((depth, b, d3), F32),
        compiler_params=_cparams(("arbitrary", "arbitrary")),
        name="ada_mod",
    )(c, ada_w, ada_b.reshape(depth, 1, d3))


def _rope_table_kernel(pos_ref, invr_ref, invm_ref, maskm_ref, cr_ref, sr_ref, cm_ref, sm_ref):
    pos = pos_ref[0].astype(F32)
    ang_r = pos * invr_ref[...]
    cr_ref[0] = jnp.cos(ang_r)
    sr_ref[0] = jnp.sin(ang_r)
    ang_m = pos * invm_ref[...]
    mask = maskm_ref[...]
    cm_ref[0] = jnp.cos(ang_m) * mask
    sm_ref[0] = jnp.sin(ang_m) * mask


def _rope_tables(positions):
    b, s = positions.shape
    t = min(s, 512)
    half_r = RET_HEAD_DIM // 2
    half_m = MLA_ROPE // 2
    inv_r = ROPE_THETA ** (-jnp.arange(half_r, dtype=F32) / half_r)
    inv_m = ROPE_THETA ** (-jnp.arange(half_m, dtype=F32) / half_m)
    invr_row = jnp.tile(inv_r, LANE // half_r).reshape(1, LANE)
    invm_row = jnp.tile(inv_m, LANE // half_m).reshape(1, LANE)
    maskm_row = (jnp.arange(LANE) < MLA_ROPE).astype(F32).reshape(1, LANE)
    row = pl.BlockSpec((1, LANE), lambda i, j: (0, 0))
    tab = pl.BlockSpec((1, t, LANE), lambda i, j: (i, j, 0))
    shp = jax.ShapeDtypeStruct((b, s, LANE), F32)
    return pl.pallas_call(
        _rope_table_kernel,
        grid=(b, s // t),
        in_specs=[pl.BlockSpec((1, t, 1), lambda i, j: (i, j, 0)), row, row, row],
        out_specs=[tab, tab, tab, tab],
        out_shape=[shp, shp, shp, shp],
        compiler_params=_cparams(("arbitrary", "arbitrary")),
        name="rope_tables",
    )(positions.reshape(b, s, 1), invr_row, invm_row, maskm_row)


IN_NCHUNK = 512


def _in_proj_kernel(x_ref, mod_ref, nw_ref, w_ref, o_ref):
    x = x_ref[0]
    shift = mod_ref[0, 0:1, :]
    scale = mod_ref[0, 1:2, :]
    y = x * lax.rsqrt(jnp.mean(x * x, axis=-1, keepdims=True) + EPS)
    y = y * nw_ref[...]
    h = (y * (1.0 + scale) + shift).astype(BF16)
    ncols = o_ref.shape[-1]
    for c0 in range(0, ncols, IN_NCHUNK):
        c1 = min(c0 + IN_NCHUNK, ncols)
        o_ref[0, :, c0:c1] = _dot(h, w_ref[:, c0:c1]).astype(BF16)


def _in_proj(x, mod3, norm_w_row, w_packed, t):
    b, s, d = x.shape
    ncols = w_packed.shape[1]
    return pl.pallas_call(
        _in_proj_kernel,
        grid=(b, s // t),
        in_specs=[
            pl.BlockSpec((1, t, d), lambda i, j: (i, j, 0)),
            pl.BlockSpec((1, 3, d), lambda i, j: (i, 0, 0)),
            pl.BlockSpec((1, d), lambda i, j: (0, 0)),
            pl.BlockSpec((d, ncols), lambda i, j: (0, 0)),
        ],
        out_specs=pl.BlockSpec((1, t, ncols), lambda i, j: (i, j, 0)),
        out_shape=jax.ShapeDtypeStruct((b, s, ncols), BF16),
        compiler_params=_cparams(("arbitrary", "arbitrary")),
        name="in_proj",
    )(x, mod3, norm_w_row, w_packed)


def _retention_kernel(q_ref, k_ref, v_ref, z_ref, cos_ref, sin_ref, dmat_ref, qw_ref, kw_ref,
                      sdec_ref, bd_ref, ones_ref, o_ref, state_ref):
    t = q_ref.shape[1]

    @pl.when(pl.program_id(1) == 0)
    def _():
        state_ref[...] = jnp.zeros_like(state_ref)

    cos = cos_ref[0]
    sin = sin_ref[0]

    def rope(ref):
        a = ref[0].astype(F32)
        x1 = a[:, :LANE]
        x2 = a[:, LANE:]
        return jnp.concatenate([x1 * cos - x2 * sin, x2 * cos + x1 * sin], axis=-1)

    q = rope(q_ref)
    k = rope(k_ref) * (RET_HEAD_DIM ** -0.5)
    v = v_ref[0]

    lane = lax.broadcasted_iota(jnp.int32, (1, RET_WIDTH), 1)
    qk_head = (lane % LANE) // (RET_HEAD_DIM // 2)
    v_head = lane // RET_HEAD_DIM
    zero_bf = jnp.zeros((), BF16)
    k_bf = k.astype(BF16)
    k_stack = jnp.concatenate(
        [jnp.where(qk_head == h, k_bf, zero_bf) for h in range(RET_HEADS)], axis=0)
    v_stack = jnp.concatenate(
        [jnp.where(v_head == h, v, zero_bf) for h in range(RET_HEADS)], axis=0)

    scores = _dot_nt(q.astype(BF16), k_stack) * dmat_ref[...]
    intra = _dot(scores.astype(BF16), v_stack)

    state = state_ref[...]
    inter = _dot((q * qw_ref[...]).astype(BF16), state.astype(BF16))
    u = _dot_tn((k * kw_ref[...]).astype(BF16), v) * bd_ref[...]
    state_ref[...] = state * sdec_ref[...] + u

    o = intra + inter
    ms = _dot((o * o).astype(BF16), ones_ref[...])
    y = o * lax.rsqrt(ms + EPS)
    z = z_ref[0].astype(F32)
    o_ref[0] = (y * _silu(z)).astype(BF16)


def _retention_consts(t):
    f32 = F32
    h = RET_HEADS
    log_gamma = jnp.log1p(-jnp.exp2(-5.0 - jnp.arange(h, dtype=f32)))
    idx = jnp.arange(t, dtype=f32)
    dist = jnp.abs(idx[:, None] - idx[None, :])
    ci = jnp.arange(t) // CHUNK
    vis = (ci[None, :] <= ci[:, None]).astype(f32)
    dmat = jnp.exp(log_gamma[:, None, None] * dist[None]) * vis[None]
    dmat = jnp.moveaxis(dmat, 0, 1).reshape(t, h * t)
    lane = jnp.arange(RET_WIDTH)
    qk_head = (lane % LANE) // (RET_HEAD_DIM // 2)
    v_head = lane // RET_HEAD_DIM
    lg_lane = log_gamma[qk_head]
    qw = jnp.exp((idx + 1.0)[:, None] * lg_lane[None, :])
    kw = jnp.exp((t - 1.0 - idx)[:, None] * lg_lane[None, :])
    bd = (qk_head[:, None] == v_head[None, :]).astype(f32)
    sdec = jnp.exp(t * lg_lane)[:, None] * bd
    ones = ((v_head[:, None] == v_head[None, :]).astype(f32) / RET_HEAD_DIM).astype(BF16)
    return dmat, qw, kw, sdec, bd, ones


def _retention(proj, cos_r, sin_r):
    b, s, _ = proj.shape
    t = min(RET_TILE, s)
    dmat, qw, kw, sdec, bd, ones = _retention_consts(t)
    w = RET_WIDTH

    def col(off):
        return pl.BlockSpec((1, t, w), lambda i, j, o=off // w: (i, j, o))

    def const(a):
        return pl.BlockSpec(a.shape, lambda i, j: (0, 0))

    tab = pl.BlockSpec((1, t, LANE), lambda i, j: (i, j, 0))
    return pl.pallas_call(
        _retention_kernel,
        grid=(b, s // t),
        in_specs=[col(OFF_RQ), col(OFF_RK), col(OFF_RV), col(OFF_RZ), tab, tab,
                  const(dmat), const(qw), const(kw), const(sdec), const(bd), const(ones)],
        out_specs=pl.BlockSpec((1, t, w), lambda i, j: (i, j, 0)),
        out_shape=jax.ShapeDtypeStruct((b, s, w), BF16),
        scratch_shapes=[pltpu.VMEM((w, w), F32)],
        compiler_params=_cparams(("arbitrary", "arbitrary")),
        name="retention",
    )(proj, proj, proj, proj, cos_r, sin_r, dmat, qw, kw, sdec, bd, ones)


def _gla_kernel(q_ref, k_ref, v_ref, z_ref, mb_ref, wg_ref, bg_ref, ltri_ref, bdt_ref, ones_ref,
                nw_ref, o_ref, state_ref):
    t = q_ref.shape[1]
    nchunk = t // CHUNK

    @pl.when(pl.program_id(1) == 0)
    def _():
        state_ref[...] = jnp.zeros_like(state_ref)

    g = _dot(mb_ref[0], wg_ref[...]) + bg_ref[...]
    log_a = (jnp.minimum(g, 0.0) - jnp.log1p(jnp.exp(-jnp.abs(g)))) / GLA_TAU
    la_hi = log_a.astype(BF16)
    la_lo = (log_a - la_hi.astype(F32)).astype(BF16)
    ltri = ltri_ref[...]
    cum = _dot(ltri, la_hi) + _dot(ltri, la_lo)

    q = q_ref[0].astype(F32)
    k = k_ref[0].astype(F32) * (GLA_DK ** -0.5)
    e_pos = jnp.exp(cum)
    e_neg = jnp.exp(-cum)
    q_pos = (q * e_pos).astype(BF16)
    q_neg = (q * e_neg).astype(BF16)
    k_neg = (k * e_neg).astype(BF16)
    k_pos = (k * e_pos).astype(BF16)

    lane_k = lax.broadcasted_iota(jnp.int32, (1, GLA_KWIDTH), 1) // GLA_DK
    lane_v = lax.broadcasted_iota(jnp.int32, (1, GLA_WIDTH), 1) // GLA_DV
    row_i = lax.broadcasted_iota(jnp.int32, (CHUNK, GLA_HEADS * CHUNK), 0)
    col_j = lax.broadcasted_iota(jnp.int32, (CHUNK, GLA_HEADS * CHUNK), 1) % CHUNK
    causal = row_i >= col_j
    zero_bf = jnp.zeros((), BF16)
    bdt = bdt_ref[...]
    ones = ones_ref[...]
    nw = nw_ref[...]

    def stack(a, head_of_lane):
        return jnp.concatenate(
            [jnp.where(head_of_lane == h, a, zero_bf) for h in range(GLA_HEADS)], axis=0)

    state_t = state_ref[...]
    for c in range(nchunk):
        r0, r1 = c * CHUNK, (c + 1) * CHUNK
        v_c = v_ref[0, r0:r1, :]
        past = _dot_nt(q_pos[r0:r1], stack(k_neg[r0:r1], lane_k))
        fut = _dot_nt(q_neg[r0:r1], stack(k_pos[r0:r1], lane_k))
        attn = jnp.where(causal, past, fut).astype(BF16)
        intra = _dot(attn, stack(v_c, lane_v))
        inter = _dot_nt(q_pos[r0:r1], state_t.astype(BF16))
        last = cum[r1 - 1:r1, :]
        k_st = (k[r0:r1] * jnp.exp(last - cum[r0:r1])).astype(BF16)
        u_t = _dot_tn(v_c, k_st) * bdt
        state_t = state_t * jnp.exp(last) + u_t
        o = intra + inter
        ms = _dot((o * o).astype(BF16), ones)
        y = o * lax.rsqrt(ms + EPS) * nw
        z = z_ref[0, r0:r1, :].astype(F32)
        o_ref[0, r0:r1, :] = (y * _silu(z)).astype(BF16)
    state_ref[...] = state_t


def _gla(proj, w_g2p, b_g2_row, gla_norm_row):
    b, s, _ = proj.shape
    t = min(GLA_TILE, s)
    idx = jnp.arange(t)
    ltri = ((idx[:, None] >= idx[None, :]) & (idx[:, None] // CHUNK == idx[None, :] // CHUNK)).astype(BF16)
    kh = jnp.arange(GLA_KWIDTH) // GLA_DK
    vh = jnp.arange(GLA_WIDTH) // GLA_DV
    bdt = (vh[:, None] == kh[None, :]).astype(F32)
    ones = ((vh[:, None] == vh[None, :]).astype(F32) / GLA_DV).astype(BF16)

    def col(off, w):
        return pl.BlockSpec((1, t, w), lambda i, j, o=off // w: (i, j, o))

    def const(a):
        return pl.BlockSpec(a.shape, lambda i, j: (0, 0))

    return pl.pallas_call(
        _gla_kernel,
        grid=(b, s // t),
        in_specs=[col(OFF_GQ, GLA_KWIDTH), col(OFF_GK, GLA_KWIDTH), col(OFF_GV, GLA_WIDTH),
                  col(OFF_GZ, GLA_WIDTH), col(OFF_MB, LANE),
                  const(w_g2p), const(b_g2_row), const(ltri), const(bdt), const(ones),
                  const(gla_norm_row)],
        out_specs=pl.BlockSpec((1, t, GLA_WIDTH), lambda i, j: (i, j, 0)),
        out_shape=jax.ShapeDtypeStruct((b, s, GLA_WIDTH), BF16),
        scratch_shapes=[pltpu.VMEM((GLA_WIDTH, GLA_KWIDTH), F32)],
        compiler_params=_cparams(("arbitrary", "arbitrary")),
        name="gla",
    )(proj, proj, proj, proj, proj, w_g2p, b_g2_row, ltri, bdt, ones, gla_norm_row)


def _absorb_kernel(wq_ref, wk_ref, o_ref):
    o_ref[0, 0] = lax.dot_general(wq_ref[0, 0], wk_ref[0, 0], (((1,), (1,)), ((), ())),
                                  precision=lax.Precision.HIGHEST, preferred_element_type=F32)


def _absorb_weights(wq_nope, wk_nope):
    depth, h, r, dn = wq_nope.shape
    kr = wk_nope.shape[2]
    return pl.pallas_call(
        _absorb_kernel,
        grid=(depth, h),
        in_specs=[pl.BlockSpec((1, 1, r, dn), lambda l, i: (l, i, 0, 0)),
                  pl.BlockSpec((1, 1, kr, dn), lambda l, i: (l, i, 0, 0))],
        out_specs=pl.BlockSpec((1, 1, r, kr), lambda l, i: (l, i, 0, 0)),
        out_shape=jax.ShapeDtypeStruct((depth, h, r, kr), F32),
        compiler_params=_cparams(("arbitrary", "arbitrary")),
        name="mla_absorb",
    )(wq_nope, wk_nope)


def _mla_prep_kernel(mq_ref, kv_ref, mb_ref, mc_ref, cos_ref, sin_ref, qnw_ref, kvnw_ref,
                     wq_ref, wrot_ref, q_out, k_out, vt_out, *, n_chunks):
    cos = cos_ref[0]
    sin = sin_ref[0]
    lat = mq_ref[0].astype(F32)
    lat = lat * lax.rsqrt(jnp.mean(lat * lat, axis=-1, keepdims=True) + EPS) * qnw_ref[...]
    lat = lat.astype(BF16)
    scale = (MLA_NOPE + MLA_ROPE) ** -0.5 * LOG2E
    t = lat.shape[0]
    tok_chunk = (pl.program_id(1) * t + lax.broadcasted_iota(jnp.int32, (t, LANE), 0)) // CHUNK
    code_lane = lax.broadcasted_iota(jnp.int32, (t, LANE), 1) - MLA_ROPE
    q_code = jnp.where((code_lane > tok_chunk) & (code_lane < n_chunks), MASK_NEG, 0.0)
    k_code = jnp.where(code_lane == tok_chunk, 1.0, 0.0)
    for h in range(MLA_HEADS):
        main = _dot(lat, wq_ref[:, 2 * LANE * h:2 * LANE * (h + 1)])
        rot = _dot(lat, wrot_ref[:, LANE * h:LANE * (h + 1)])
        q_out[0, h, :, :LANE] = (main[:, :LANE] * scale).astype(BF16)
        q_out[0, h, :, LANE:] = ((main[:, LANE:] * cos + rot * sin) * scale + q_code).astype(BF16)
    kv = kv_ref[0].astype(F32)
    kvn = kv * lax.rsqrt(jnp.mean(kv * kv, axis=-1, keepdims=True) + EPS) * kvnw_ref[...]
    kvn_bf = kvn.astype(BF16)
    k_out[0, :, :LANE] = kvn_bf
    k_out[0, :, LANE:] = (mb_ref[0].astype(F32) * cos + mc_ref[0].astype(F32) * sin + k_code).astype(BF16)
    vt_out[0, 0:LANE, :] = kvn.T.astype(BF16)
    pad_rows = vt_out.shape[1] - LANE
    row = lax.broadcasted_iota(jnp.int32, (pad_rows, kv.shape[0]), 0)
    vt_out[0, LANE:, :] = jnp.where(row == 0, 1.0, 0.0).astype(BF16)


def _mla_prep(proj, cos_m, sin_m, qn_row, kvn_row, wq_all, wrot):
    b, s, _ = proj.shape
    t = min(512, s)

    def col(off, w):
        return pl.BlockSpec((1, t, w), lambda i, j, o=off // w: (i, j, o))

    def const(a):
        return pl.BlockSpec(a.shape, lambda i, j: (0, 0))

    tab = pl.BlockSpec((1, t, LANE), lambda i, j: (i, j, 0))
    n_chunks = s // CHUNK
    assert MLA_ROPE + n_chunks <= LANE, "chunk-mask code does not fit the spare contraction lanes"
    return pl.pallas_call(
        functools.partial(_mla_prep_kernel, n_chunks=n_chunks),
        grid=(b, s // t),
        in_specs=[col(OFF_MQ, MLA_Q_RANK), col(OFF_KV, LANE), col(OFF_MB, LANE), col(OFF_MC, LANE),
                  tab, tab, const(qn_row), const(kvn_row), const(wq_all), const(wrot)],
        out_specs=[pl.BlockSpec((1, MLA_HEADS, t, 2 * LANE), lambda i, j: (i, 0, j, 0)),
                   pl.BlockSpec((1, t, 2 * LANE), lambda i, j: (i, j, 0)),
                   pl.BlockSpec((1, VT_ROWS, t), lambda i, j: (i, 0, j))],
        out_shape=[jax.ShapeDtypeStruct((b, MLA_HEADS, s, 2 * LANE), BF16),
                   jax.ShapeDtypeStruct((b, s, 2 * LANE), BF16),
                   jax.ShapeDtypeStruct((b, VT_ROWS, s), BF16)],
        compiler_params=_cparams(("arbitrary", "arbitrary")),
        name="mla_prep",
    )(proj, proj, proj, proj, cos_m, sin_m, qn_row, kvn_row, wq_all, wrot)


def _mla_attn_kernel(q_ref, k_ref, vt_ref, z_ref, wuv_ref, o_ref, m_ref, acc_ref,
                     s0_ref, s1_ref, mt0_ref, mt1_ref):
    tq = q_ref.shape[2]
    t = ATT_T
    n_tiles = pl.program_id(1) + 1

    m_ref[...] = jnp.full_like(m_ref, -jnp.inf)
    acc_ref[...] = jnp.zeros_like(acc_ref)

    def scores(kt, s_ref, mt_ref):
        k0 = pl.multiple_of(jnp.minimum(kt, n_tiles - 1) * t, t)
        k = k_ref[0, pl.ds(k0, t), :]
        for h in range(MLA_HEADS):
            cols = slice(h * tq, (h + 1) * tq)
            s = _dot_nt(k, q_ref[0, h])
            s_ref[:, cols] = s
            mt_ref[:, cols] = jnp.max(s, axis=0, keepdims=True)

    def accumulate(kt, s_ref, mt_ref):
        k0 = pl.multiple_of(kt * t, t)
        vt = vt_ref[0, :, pl.ds(k0, t)]
        for h in range(MLA_HEADS):
            cols = slice(h * tq, (h + 1) * tq)
            m_prev = m_ref[:, cols]
            m_new = jnp.maximum(m_prev, mt_ref[:, cols])
            alpha = jnp.exp2(m_prev - m_new)
            p = jnp.exp2(s_ref[:, cols] - m_new).astype(BF16)
            acc_ref[:, cols] = alpha * acc_ref[:, cols] + _dot(vt, p)
            m_ref[:, cols] = m_new

    scores(0, s0_ref, mt0_ref)

    def pair(j, carry):
        scores(2 * j + 1, s1_ref, mt1_ref)
        accumulate(2 * j, s0_ref, mt0_ref)
        scores(2 * j + 2, s0_ref, mt0_ref)
        accumulate(2 * j + 1, s1_ref, mt1_ref)
        return carry

    lax.fori_loop(0, n_tiles // 2, pair, 0)

    @pl.when(n_tiles % 2 == 1)
    def _():
        accumulate(n_tiles - 1, s0_ref, mt0_ref)

    outs = []
    for h in range(MLA_HEADS):
        cols = slice(h * tq, (h + 1) * tq)
        o_h = (acc_ref[0:LANE, cols] / acc_ref[LANE:LANE + 1, cols]).astype(BF16)
        outs.append(_dot(wuv_ref[h], o_h))
    o_t = jnp.concatenate(outs, axis=0)
    z = z_ref[0].astype(F32)
    o_ref[0] = (o_t.T * _silu(z)).astype(BF16)


def _mla_attn(q_s, k_c, v_t, proj, wuv):
    b, h, s, _ = q_s.shape
    t = ATT_T
    nq = h * t
    vrows = v_t.shape[1]
    return pl.pallas_call(
        _mla_attn_kernel,
        grid=(b, s // t),
        in_specs=[pl.BlockSpec((1, h, t, 2 * LANE), lambda i, j: (i, 0, j, 0)),
                  pl.BlockSpec((1, s, 2 * LANE), lambda i, j: (i, 0, 0)),
                  pl.BlockSpec((1, vrows, s), lambda i, j: (i, 0, 0)),
                  pl.BlockSpec((1, t, MLA_WIDTH), lambda i, j: (i, j, OFF_MZ // MLA_WIDTH)),
                  pl.BlockSpec(wuv.shape, lambda i, j: (0, 0, 0))],
        out_specs=pl.BlockSpec((1, t, MLA_WIDTH), lambda i, j: (i, j, 0)),
        out_shape=jax.ShapeDtypeStruct((b, s, MLA_WIDTH), BF16),
        scratch_shapes=[pltpu.VMEM((1, nq), F32), pltpu.VMEM((vrows, nq), F32),
                        pltpu.VMEM((t, nq), F32), pltpu.VMEM((t, nq), F32),
                        pltpu.VMEM((1, nq), F32), pltpu.VMEM((1, nq), F32)],
        compiler_params=_cparams(("arbitrary", "arbitrary")),
        name="mla_attn",
    )(q_s, k_c, v_t, proj, wuv)


def _out_proj_kernel(r_ref, m_ref, g_ref, x_ref, mod_ref, w_ref, fw_ref, o_ref, *, final):
    y = _dot(r_ref[0], w_ref[0:RET_WIDTH, :])
    y += _dot(m_ref[0], w_ref[RET_WIDTH:RET_WIDTH + MLA_WIDTH, :])
    y += _dot(g_ref[0], w_ref[RET_WIDTH + MLA_WIDTH:, :])
    gate = mod_ref[0, 2:3, :]
    x = x_ref[0] + gate * y
    if final:
        x = x * lax.rsqrt(jnp.mean(x * x, axis=-1, keepdims=True) + EPS) * fw_ref[...]
    o_ref[0] = x


def _out_proj(r_o, m_o, g_o, x, mod3, w_out_bf, final_row, final, t):
    b, s, d = x.shape

    def tok(w):
        return pl.BlockSpec((1, t, w), lambda i, j: (i, j, 0))

    return pl.pallas_call(
        functools.partial(_out_proj_kernel, final=final),
        grid=(b, s // t),
        in_specs=[tok(RET_WIDTH), tok(MLA_WIDTH), tok(GLA_WIDTH), tok(d),
                  pl.BlockSpec((1, 3, d), lambda i, j: (i, 0, 0)),
                  pl.BlockSpec(w_out_bf.shape, lambda i, j: (0, 0)),
                  pl.BlockSpec((1, d), lambda i, j: (0, 0))],
        out_specs=tok(d),
        out_shape=jax.ShapeDtypeStruct((b, s, d), F32),
        compiler_params=_cparams(("arbitrary", "arbitrary")),
        name="out_proj",
    )(r_o, m_o, g_o, x, mod3, w_out_bf, final_row)


def _pack_w_in(w_in_l):
    src = np.full((NP_COLS,), -1, np.int64)
    sign = np.ones((NP_COLS,), np.float32)
    ret0, mla0, gla0 = 0, 4 * RET_WIDTH, 4 * RET_WIDTH + (MLA_Q_RANK + MLA_KV_RANK + MLA_ROPE + MLA_WIDTH)
    half = RET_HEAD_DIM // 2
    for p in range(LANE):
        hh, i = p // half, p % half
        for base, off in ((ret0, OFF_RQ), (ret0 + RET_WIDTH, OFF_RK)):
            src[off + p] = base + hh * RET_HEAD_DIM + i
            src[off + LANE + p] = base + hh * RET_HEAD_DIM + half + i
    src[OFF_RV:OFF_RV + RET_WIDTH] = ret0 + 2 * RET_WIDTH + np.arange(RET_WIDTH)
    src[OFF_RZ:OFF_RZ + RET_WIDTH] = ret0 + 3 * RET_WIDTH + np.arange(RET_WIDTH)
    src[OFF_MQ:OFF_MQ + MLA_Q_RANK] = mla0 + np.arange(MLA_Q_RANK)
    src[OFF_KV:OFF_KV + MLA_KV_RANK] = mla0 + MLA_Q_RANK + np.arange(MLA_KV_RANK)
    kr0 = mla0 + MLA_Q_RANK + MLA_KV_RANK
    src[OFF_MB:OFF_MB + MLA_ROPE] = kr0 + np.arange(MLA_ROPE)
    hm = MLA_ROPE // 2
    src[OFF_MC:OFF_MC + hm] = kr0 + hm + np.arange(hm)
    sign[OFF_MC:OFF_MC + hm] = -1.0
    src[OFF_MC + hm:OFF_MC + MLA_ROPE] = kr0 + np.arange(hm)
    src[OFF_MZ:OFF_MZ + MLA_WIDTH] = kr0 + MLA_ROPE + np.arange(MLA_WIDTH)
    src[OFF_GQ:OFF_GQ + GLA_KWIDTH] = gla0 + np.arange(GLA_KWIDTH)
    src[OFF_GK:OFF_GK + GLA_KWIDTH] = gla0 + GLA_KWIDTH + np.arange(GLA_KWIDTH)
    src[OFF_GV:OFF_GV + GLA_WIDTH] = gla0 + 2 * GLA_KWIDTH + np.arange(GLA_WIDTH)
    gg0 = gla0 + 2 * GLA_KWIDTH + GLA_WIDTH
    src[OFF_MB + MLA_ROPE:OFF_MB + MLA_ROPE + GLA_GATE_RANK] = gg0 + np.arange(GLA_GATE_RANK)
    src[OFF_GZ:OFF_GZ + GLA_WIDTH] = gg0 + GLA_GATE_RANK + np.arange(GLA_WIDTH)
    valid = src >= 0
    gathered = jnp.take(w_in_l, jnp.asarray(np.where(valid, src, 0)), axis=1)
    packed = gathered * jnp.asarray(np.where(valid, sign, 0.0))[None, :]
    return packed.astype(BF16)


def _pack_mla_q(w_uq_l, w_abs_l):
    r = w_uq_l.shape[0]
    hd = MLA_NOPE + MLA_ROPE
    hm = MLA_ROPE // 2
    w3 = w_uq_l.reshape(r, MLA_HEADS, hd)
    pe = w3[:, :, MLA_NOPE:]
    pad = jnp.zeros((r, MLA_HEADS, LANE - MLA_ROPE), F32)
    main = jnp.concatenate([jnp.moveaxis(w_abs_l, 0, 1), pe, pad], axis=-1)
    rot = jnp.concatenate([-pe[:, :, hm:], pe[:, :, :hm], pad], axis=-1)
    return (main.reshape(r, MLA_HEADS * 2 * LANE).astype(BF16),
            rot.reshape(r, MLA_HEADS * LANE).astype(BF16))


def kernel(x, c, positions, norm_w, ada_w, ada_b, w_in, mla_q_norm, w_uq, mla_kv_norm, w_ukv,
           gla_w_g2, gla_b_g2, gla_norm, w_out, final_norm):
    b, s, d = x.shape
    depth = w_in.shape[0]
    t_tok = min(512, s)

    mod = _ada_mod(c, ada_w, ada_b).reshape(depth, b, 3, d)
    cos_r, sin_r, cos_m, sin_m = _rope_tables(positions)

    kv_hd = MLA_NOPE + MLA_V
    q_hd = MLA_NOPE + MLA_ROPE
    w_ukv4 = w_ukv.reshape(depth, MLA_KV_RANK, MLA_HEADS, kv_hd)
    wk_nope = jnp.moveaxis(w_ukv4[..., :MLA_NOPE], 2, 1)
    wuv = jnp.transpose(w_ukv4[..., MLA_NOPE:], (0, 2, 3, 1)).astype(BF16)
    wq_nope = jnp.moveaxis(
        w_uq.reshape(depth, MLA_Q_RANK, MLA_HEADS, q_hd)[..., :MLA_NOPE], 2, 1)
    w_abs = _absorb_weights(wq_nope, wk_nope)

    final_row = final_norm.reshape(1, d)
    for l in range(depth):
        w_packed = _pack_w_in(w_in[l])
        wq_all, wrot = _pack_mla_q(w_uq[l], w_abs[l])
        w_g2p = jnp.zeros((LANE, GLA_KWIDTH), F32).at[MLA_ROPE:MLA_ROPE + GLA_GATE_RANK].set(
            gla_w_g2[l]).astype(BF16)

        proj = _in_proj(x, mod[l], norm_w[l].reshape(1, d), w_packed, t_tok)
        r_o = _retention(proj, cos_r, sin_r)
        g_o = _gla(proj, w_g2p, gla_b_g2[l].reshape(1, GLA_KWIDTH),
                   jnp.tile(gla_norm[l], GLA_HEADS).reshape(1, GLA_WIDTH))
        q_s, k_c, v_t = _mla_prep(proj, cos_m, sin_m, mla_q_norm[l].reshape(1, MLA_Q_RANK),
                                  mla_kv_norm[l].reshape(1, MLA_KV_RANK), wq_all, wrot)
        m_o = _mla_attn(q_s, k_c, v_t, proj, wuv[l])
        x = _out_proj(r_o, m_o, g_o, x, mod[l], w_out[l].astype(BF16), final_row,
                      l == depth - 1, t_tok)
    return x
```

```python
import functools

import numpy as np

import jax
import jax.numpy as jnp
from jax import lax
from jax.experimental import pallas as pl
from jax.experimental.pallas import tpu as pltpu

F32 = jnp.float32
BF16 = jnp.bfloat16

D_MODEL = 1024
CHUNK = 64
EPS = 1e-6
ROPE_THETA = 10000.0

RET_HEADS = 4
RET_HEAD_DIM = 64
RET_WIDTH = 256
MLA_HEADS = 8
MLA_NOPE = 64
MLA_ROPE = 32
MLA_V = 64
MLA_WIDTH = 512
MLA_Q_RANK = 256
MLA_KV_RANK = 128
GLA_HEADS = 4
GLA_DK = 32
GLA_DV = 64
GLA_KWIDTH = 128
GLA_WIDTH = 256
GLA_GATE_RANK = 16
GLA_TAU = 16.0
IN_COLS = 2736

LANE = 128

OFF_MZ = 0
OFF_RQ = 512
OFF_RK = 768
OFF_RV = 1024
OFF_RZ = 1280
OFF_MQ = 1536
OFF_GV = 1792
OFF_GZ = 2048
OFF_KV = 2304
OFF_MB = 2432
OFF_MC = 2560
OFF_GQ = 2688
OFF_GK = 2816
NP_COLS = 2944

RET_TILE = 256
RET_SEQS = 4
GLA_TILE = 256
GLA_SEQS = 4
ATT_T = 256
VT_ROWS = 144
LOG2E = 1.4426950408889634
MASK_NEG = -1e30
VMEM_LIMIT = 56 * 1024 * 1024


def _cparams(sem, flags=None):
    return pltpu.CompilerParams(dimension_semantics=sem, vmem_limit_bytes=VMEM_LIMIT, flags=flags)


def _dot(a, b):
    return jnp.dot(a, b, preferred_element_type=F32)


def _dot_nt(a, b):
    return lax.dot_general(a, b, (((1,), (1,)), ((), ())), preferred_element_type=F32)


def _dot_tn(a, b):
    return lax.dot_general(a, b, (((0,), (0,)), ((), ())), preferred_element_type=F32)


def _silu(x):
    return x / (1.0 + jnp.exp(-x))


def _ada_kernel(c_ref, w_ref, b_ref, o_ref):
    c = c_ref[...]
    o_ref[0] = _dot(_silu(c), w_ref[0]) + b_ref[0]


def _ada_mod(c, ada_w, ada_b):
    depth, d, d3 = ada_w.shape
    b = c.shape[0]
    nblk = d3 // d
    return pl.pallas_call(
        _ada_kernel,
        grid=(depth, nblk),
        in_specs=[
            pl.BlockSpec((b, d), lambda l, j: (0, 0)),
            pl.BlockSpec((1, d, d), lambda l, j: (l, 0, j)),
            pl.BlockSpec((1, 1, d), lambda l, j: (l, 0, j)),
        ],
        out_specs=pl.BlockSpec((1, b, d), lambda l, j: (l, 0, j)),
        out_shape=jax.ShapeDtypeStruct((depth, b, d3), F32),
        compiler_params=_cparams(("arbitrary", "arbitrary")),
        name="ada_mod",
    )(c, ada_w, ada_b.reshape(depth, 1, d3))


def _rope_table_kernel(pos_ref, inv_ref, cr_ref, sr_ref, cm_ref, sm_ref):
    half_r = RET_HEAD_DIM // 2
    half_m = MLA_ROPE // 2
    pos = pos_ref[0].astype(F32)
    ang = pos * inv_ref[...]
    lane = lax.broadcasted_iota(jnp.int32, ang.shape, 1)
    is_r = lane < half_r
    is_m = (lane >= half_r) & (lane < half_r + half_m)

    def tables(x):
        r = jnp.where(is_r, x, 0.0)
        r = r + pltpu.roll(r, half_r, 1)
        r = r + pltpu.roll(r, 2 * half_r, 1)
        m = pltpu.roll(jnp.where(is_m, x, 0.0), LANE - half_r, 1)
        m = m + pltpu.roll(m, half_m, 1)
        return r, m

    cr_ref[0], cm_ref[0] = tables(jnp.cos(ang))
    sr_ref[0], sm_ref[0] = tables(jnp.sin(ang))


def _rope_tables(positions):
    b, s = positions.shape
    t = min(s, 512)
    half_r = RET_HEAD_DIM // 2
    half_m = MLA_ROPE // 2
    inv_r = ROPE_THETA ** (-jnp.arange(half_r, dtype=F32) / half_r)
    inv_m = ROPE_THETA ** (-jnp.arange(half_m, dtype=F32) / half_m)
    inv_row = jnp.concatenate(
        [inv_r, inv_m, jnp.zeros((LANE - half_r - half_m,), F32)]).reshape(1, LANE)
    tab = pl.BlockSpec((1, t, LANE), lambda i, j: (i, j, 0))
    shp = jax.ShapeDtypeStruct((b, s, LANE), F32)
    return pl.pallas_call(
        _rope_table_kernel,
        grid=(b, s // t),
        in_specs=[pl.BlockSpec((1, t, 1), lambda i, j: (i, j, 0)),
                  pl.BlockSpec((1, LANE), lambda i, j: (0, 0))],
        out_specs=[tab, tab, tab, tab],
        out_shape=[shp, shp, shp, shp],
        compiler_params=_cparams(("arbitrary", "arbitrary")),
        name="rope_tables",
    )(positions.reshape(b, s, 1), inv_row)


IN_NCHUNK = 512


def _in_proj_kernel(x_ref, mod_ref, nw_ref, w_ref, o_ref):
    x = x_ref[0]
    shift = mod_ref[0, 0:1, :]
    scale = mod_ref[0, 1:2, :]
    y = x * lax.rsqrt(jnp.mean(x * x, axis=-1, keepdims=True) + EPS)
    y = y * nw_ref[...]
    h = (y * (1.0 + scale) + shift).astype(BF16)
    ncols = o_ref.shape[-1]
    for c0 in range(0, ncols, IN_NCHUNK):
        c1 = min(c0 + IN_NCHUNK, ncols)
        o_ref[0, :, c0:c1] = _dot(h, w_ref[:, c0:c1]).astype(BF16)


def _in_proj(x, mod3, norm_w_row, w_packed, t):
    b, s, d = x.shape
    ncols = w_packed.shape[1]
    return pl.pallas_call(
        _in_proj_kernel,
        grid=(b, s // t),
        in_specs=[
            pl.BlockSpec((1, t, d), lambda i, j: (i, j, 0)),
            pl.BlockSpec((1, 3, d), lambda i, j: (i, 0, 0)),
            pl.BlockSpec((1, d), lambda i, j: (0, 0)),
            pl.BlockSpec((d, ncols), lambda i, j: (0, 0)),
        ],
        out_specs=pl.BlockSpec((1, t, ncols), lambda i, j: (i, j, 0)),
        out_shape=jax.ShapeDtypeStruct((b, s, ncols), BF16),
        compiler_params=_cparams(("arbitrary", "arbitrary")),
        name="in_proj",
    )(x, mod3, norm_w_row, w_packed)


def _retention_kernel(q_ref, k_ref, v_ref, z_ref, cos_ref, sin_ref, dmat_ref, qw_ref, kw_ref,
                      sdec_ref, bd_ref, ones_ref, o_ref, state_ref):
    nb = q_ref.shape[0]

    @pl.when(pl.program_id(1) == 0)
    def _():
        state_ref[...] = jnp.zeros_like(state_ref)

    def rope(ref, i):
        a = ref[i].astype(F32)
        cos, sin = cos_ref[i], sin_ref[i]
        x1 = a[:, :LANE]
        x2 = a[:, LANE:]
        return jnp.concatenate([x1 * cos - x2 * sin, x2 * cos + x1 * sin], axis=-1)

    lane = lax.broadcasted_iota(jnp.int32, (1, RET_WIDTH), 1)
    qk_head = (lane % LANE) // (RET_HEAD_DIM // 2)
    v_head = lane // RET_HEAD_DIM
    zero_bf = jnp.zeros((), BF16)

    def stack(a, head_of_lane):
        return jnp.concatenate(
            [jnp.where(head_of_lane == h, a, zero_bf) for h in range(RET_HEADS)], axis=0)

    seqs = range(nb)
    q = [rope(q_ref, i) for i in seqs]
    k = [rope(k_ref, i) * (RET_HEAD_DIM ** -0.5) for i in seqs]
    v = [v_ref[i] for i in seqs]
    state = [state_ref[i] for i in seqs]
    u = [_dot_tn((k[i] * kw_ref[...]).astype(BF16), v[i]) for i in seqs]
    inter = [_dot((q[i] * qw_ref[...]).astype(BF16), state[i].astype(BF16)) for i in seqs]
    scores = [_dot_nt(q[i].astype(BF16), stack(k[i].astype(BF16), qk_head)) for i in seqs]
    for i in seqs:
        state_ref[i] = state[i] * sdec_ref[...] + u[i] * bd_ref[...]
    intra = [_dot((scores[i] * dmat_ref[...]).astype(BF16), stack(v[i], v_head)) for i in seqs]
    o = [intra[i] + inter[i] for i in seqs]
    ms = [_dot((o[i] * o[i]).astype(BF16), ones_ref[...]) for i in seqs]
    for i in seqs:
        y = o[i] * lax.rsqrt(ms[i] + EPS)
        z = z_ref[i].astype(F32)
        o_ref[i] = (y * _silu(z)).astype(BF16)


def _retention_consts(t):
    f32 = F32
    h = RET_HEADS
    log_gamma = jnp.log1p(-jnp.exp2(-5.0 - jnp.arange(h, dtype=f32)))
    idx = jnp.arange(t, dtype=f32)
    dist = jnp.abs(idx[:, None] - idx[None, :])
    ci = jnp.arange(t) // CHUNK
    vis = (ci[None, :] <= ci[:, None]).astype(f32)
    dmat = jnp.exp(log_gamma[:, None, None] * dist[None]) * vis[None]
    dmat = jnp.moveaxis(dmat, 0, 1).reshape(t, h * t)
    lane = jnp.arange(RET_WIDTH)
    qk_head = (lane % LANE) // (RET_HEAD_DIM // 2)
    v_head = lane // RET_HEAD_DIM
    lg_lane = log_gamma[qk_head]
    qw = jnp.exp((idx + 1.0)[:, None] * lg_lane[None, :])
    kw = jnp.exp((t - 1.0 - idx)[:, None] * lg_lane[None, :])
    bd = (qk_head[:, None] == v_head[None, :]).astype(f32)
    sdec = jnp.exp(t * lg_lane)[:, None] * bd
    ones = ((v_head[:, None] == v_head[None, :]).astype(f32) / RET_HEAD_DIM).astype(BF16)
    return dmat, qw, kw, sdec, bd, ones


def _retention(proj, cos_r, sin_r):
    b, s, _ = proj.shape
    t = min(RET_TILE, s)
    dmat, qw, kw, sdec, bd, ones = _retention_consts(t)
    w = RET_WIDTH

    nb = RET_SEQS if b % RET_SEQS == 0 else 1

    def col(off):
        return pl.BlockSpec((nb, t, w), lambda i, j, o=off // w: (i, j, o))

    def const(a):
        return pl.BlockSpec(a.shape, lambda i, j: (0, 0))

    tab = pl.BlockSpec((nb, t, LANE), lambda i, j: (i, j, 0))
    return pl.pallas_call(
        _retention_kernel,
        grid=(b // nb, s // t),
        in_specs=[col(OFF_RQ), col(OFF_RK), col(OFF_RV), col(OFF_RZ), tab, tab,
                  const(dmat), const(qw), const(kw), const(sdec), const(bd), const(ones)],
        out_specs=pl.BlockSpec((nb, t, w), lambda i, j: (i, j, 0)),
        out_shape=jax.ShapeDtypeStruct((b, s, w), BF16),
        scratch_shapes=[pltpu.VMEM((nb, w, w), F32)],
        compiler_params=_cparams(("arbitrary", "arbitrary")),
        name="retention",
    )(proj, proj, proj, proj, cos_r, sin_r, dmat, qw, kw, sdec, bd, ones)


def _gla_kernel(q_ref, k_ref, v_ref, z_ref, mb_ref, wg_ref, bg_ref, ltri_ref, bdt_ref, ones_ref,
                nw_ref, o_ref, state_ref):
    nb, t = q_ref.shape[0], q_ref.shape[1]
    nchunk = t // CHUNK

    @pl.when(pl.program_id(1) == 0)
    def _():
        state_ref[...] = jnp.zeros_like(state_ref)

    lane_k = lax.broadcasted_iota(jnp.int32, (1, GLA_KWIDTH), 1) // GLA_DK
    lane_v = lax.broadcasted_iota(jnp.int32, (1, GLA_WIDTH), 1) // GLA_DV
    row_i = lax.broadcasted_iota(jnp.int32, (CHUNK, GLA_HEADS * CHUNK), 0)
    col_j = lax.broadcasted_iota(jnp.int32, (CHUNK, GLA_HEADS * CHUNK), 1) % CHUNK
    causal = row_i >= col_j
    zero_bf = jnp.zeros((), BF16)
    ltri = ltri_ref[...]
    bdt = bdt_ref[...]
    ones = ones_ref[...]
    nw = nw_ref[...]

    def stack(a, head_of_lane):
        return jnp.concatenate(
            [jnp.where(head_of_lane == h, a, zero_bf) for h in range(GLA_HEADS)], axis=0)

    seqs = range(nb)
    g = [_dot(mb_ref[i], wg_ref[...]) + bg_ref[...] for i in seqs]
    log_a = [(jnp.minimum(x, 0.0) - jnp.log1p(jnp.exp(-jnp.abs(x)))) / GLA_TAU for x in g]
    la_hi = [x.astype(BF16) for x in log_a]
    la_lo = [(x - h.astype(F32)).astype(BF16) for x, h in zip(log_a, la_hi)]
    cum = [_dot(ltri, h) + _dot(ltri, l) for h, l in zip(la_hi, la_lo)]
    q = [q_ref[i].astype(F32) for i in seqs]
    k = [k_ref[i].astype(F32) * (GLA_DK ** -0.5) for i in seqs]
    e_pos = [jnp.exp(x) for x in cum]
    e_neg = [jnp.exp(-x) for x in cum]
    q_pos = [(a * e).astype(BF16) for a, e in zip(q, e_pos)]
    q_neg = [(a * e).astype(BF16) for a, e in zip(q, e_neg)]
    k_neg = [(a * e).astype(BF16) for a, e in zip(k, e_neg)]
    k_pos = [(a * e).astype(BF16) for a, e in zip(k, e_pos)]

    units = [(i, slice(c * CHUNK, (c + 1) * CHUNK)) for i in seqs for c in range(nchunk)]
    v_c = [v_ref[i, r, :] for i, r in units]
    last = [cum[i][r.stop - 1:r.stop, :] for i, r in units]
    k_st = [(k[i][r] * jnp.exp(l - cum[i][r])).astype(BF16) for (i, r), l in zip(units, last)]
    u_t = [_dot_tn(vc, ks) for vc, ks in zip(v_c, k_st)]
    past = [_dot_nt(q_pos[i][r], stack(k_neg[i][r], lane_k)) for i, r in units]
    fut = [_dot_nt(q_neg[i][r], stack(k_pos[i][r], lane_k)) for i, r in units]
    attn = [jnp.where(causal, p, f).astype(BF16) for p, f in zip(past, fut)]
    intra = [_dot(a, stack(vc, lane_v)) for a, vc in zip(attn, v_c)]
    states = []
    for i in seqs:
        st = state_ref[i]
        for c in range(nchunk):
            states.append(st)
            st = st * jnp.exp(last[i * nchunk + c]) + u_t[i * nchunk + c] * bdt
        state_ref[i] = st
    inter = [_dot_nt(q_pos[i][r], st.astype(BF16)) for (i, r), st in zip(units, states)]
    o = [a + b for a, b in zip(intra, inter)]
    ms = [_dot((x * x).astype(BF16), ones) for x in o]
    for (i, r), x, m in zip(units, o, ms):
        y = x * lax.rsqrt(m + EPS) * nw
        z = z_ref[i, r, :].astype(F32)
        o_ref[i, r, :] = (y * _silu(z)).astype(BF16)


def _gla(proj, w_g2p, b_g2_row, gla_norm_row):
    b, s, _ = proj.shape
    t = min(GLA_TILE, s)
    idx = jnp.arange(t)
    ltri = ((idx[:, None] >= idx[None, :]) & (idx[:, None] // CHUNK == idx[None, :] // CHUNK)).astype(BF16)
    kh = jnp.arange(GLA_KWIDTH) // GLA_DK
    vh = jnp.arange(GLA_WIDTH) // GLA_DV
    bdt = (vh[:, None] == kh[None, :]).astype(F32)
    ones = ((vh[:, None] == vh[None, :]).astype(F32) / GLA_DV).astype(BF16)

    nb = GLA_SEQS if b % GLA_SEQS == 0 else 1

    def col(off, w):
        return pl.BlockSpec((nb, t, w), lambda i, j, o=off // w: (i, j, o))

    def const(a):
        return pl.BlockSpec(a.shape, lambda i, j: (0, 0))

    return pl.pallas_call(
        _gla_kernel,
        grid=(b // nb, s // t),
        in_specs=[col(OFF_GQ, GLA_KWIDTH), col(OFF_GK, GLA_KWIDTH), col(OFF_GV, GLA_WIDTH),
                  col(OFF_GZ, GLA_WIDTH), col(OFF_MB, LANE),
                  const(w_g2p), const(b_g2_row), const(ltri), const(bdt), const(ones),
                  const(gla_norm_row)],
        out_specs=pl.BlockSpec((nb, t, GLA_WIDTH), lambda i, j: (i, j, 0)),
        out_shape=jax.ShapeDtypeStruct((b, s, GLA_WIDTH), BF16),
        scratch_shapes=[pltpu.VMEM((nb, GLA_WIDTH, GLA_KWIDTH), F32)],
        compiler_params=_cparams(("arbitrary", "arbitrary")),
        name="gla",
    )(proj, proj, proj, proj, proj, w_g2p, b_g2_row, ltri, bdt, ones, gla_norm_row)


def _absorb_kernel(wq_ref, wk_ref, o_ref):
    o_ref[0, 0] = lax.dot_general(wq_ref[0, 0], wk_ref[0, 0], (((1,), (1,)), ((), ())),
                                  precision=lax.Precision.HIGHEST, preferred_element_type=F32)


def _absorb_weights(wq_nope, wk_nope):
    depth, h, r, dn = wq_nope.shape
    kr = wk_nope.shape[2]
    return pl.pallas_call(
        _absorb_kernel,
        grid=(depth, h),
        in_specs=[pl.BlockSpec((1, 1, r, dn), lambda l, i: (l, i, 0, 0)),
                  pl.BlockSpec((1, 1, kr, dn), lambda l, i: (l, i, 0, 0))],
        out_specs=pl.BlockSpec((1, 1, r, kr), lambda l, i: (l, i, 0, 0)),
        out_shape=jax.ShapeDtypeStruct((depth, h, r, kr), F32),
        compiler_params=_cparams(("arbitrary", "arbitrary")),
        name="mla_absorb",
    )(wq_nope, wk_nope)


def _mla_prep_kernel(mq_ref, kv_ref, mb_ref, mc_ref, cos_ref, sin_ref, qnw_ref, kvnw_ref,
                     wq_ref, wrot_ref, q_out, k_out, vt_out, *, n_chunks):
    cos = cos_ref[0]
    sin = sin_ref[0]
    lat = mq_ref[0].astype(F32)
    lat = lat * lax.rsqrt(jnp.mean(lat * lat, axis=-1, keepdims=True) + EPS) * qnw_ref[...]
    lat = lat.astype(BF16)
    scale = (MLA_NOPE + MLA_ROPE) ** -0.5 * LOG2E
    t = lat.shape[0]
    tok_chunk = (pl.program_id(1) * t + lax.broadcasted_iota(jnp.int32, (t, LANE), 0)) // CHUNK
    code_lane = lax.broadcasted_iota(jnp.int32, (t, LANE), 1) - MLA_ROPE
    q_code = jnp.where((code_lane > tok_chunk) & (code_lane < n_chunks), MASK_NEG, 0.0)
    k_code = jnp.where(code_lane == tok_chunk, 1.0, 0.0)
    for h in range(MLA_HEADS):
        main = _dot(lat, wq_ref[:, 2 * LANE * h:2 * LANE * (h + 1)])
        rot = _dot(lat, wrot_ref[:, LANE * h:LANE * (h + 1)])
        q_out[0, h, :, :LANE] = (main[:, :LANE] * scale).astype(BF16)
        q_out[0, h, :, LANE:] = ((main[:, LANE:] * cos + rot * sin) * scale + q_code).astype(BF16)
    kv = kv_ref[0].astype(F32)
    kvn = kv * lax.rsqrt(jnp.mean(kv * kv, axis=-1, keepdims=True) + EPS) * kvnw_ref[...]
    kvn_bf = kvn.astype(BF16)
    k_out[0, :, :LANE] = kvn_bf
    k_out[0, :, LANE:] = (mb_ref[0].astype(F32) * cos + mc_ref[0].astype(F32) * sin + k_code).astype(BF16)
    vt_out[0, 0:LANE, :] = kvn.T.astype(BF16)
    pad_rows = vt_out.shape[1] - LANE
    row = lax.broadcasted_iota(jnp.int32, (pad_rows, kv.shape[0]), 0)
    vt_out[0, LANE:, :] = jnp.where(row == 0, 1.0, 0.0).astype(BF16)


def _mla_prep(proj, cos_m, sin_m, qn_row, kvn_row, wq_all, wrot):
    b, s, _ = proj.shape
    t = min(512, s)

    def col(off, w):
        return pl.BlockSpec((1, t, w), lambda i, j, o=off // w: (i, j, o))

    def const(a):
        return pl.BlockSpec(a.shape, lambda i, j: (0, 0))

    tab = pl.BlockSpec((1, t, LANE), lambda i, j: (i, j, 0))
    n_chunks = s // CHUNK
    assert MLA_ROPE + n_chunks <= LANE, "chunk-mask code does not fit the spare contraction lanes"
    return pl.pallas_call(
        functools.partial(_mla_prep_kernel, n_chunks=n_chunks),
        grid=(b, s // t),
        in_specs=[col(OFF_MQ, MLA_Q_RANK), col(OFF_KV, LANE), col(OFF_MB, LANE), col(OFF_MC, LANE),
                  tab, tab, const(qn_row), const(kvn_row), const(wq_all), const(wrot)],
        out_specs=[pl.BlockSpec((1, MLA_HEADS, t, 2 * LANE), lambda i, j: (i, 0, j, 0)),
                   pl.BlockSpec((1, t, 2 * LANE), lambda i, j: (i, j, 0)),
                   pl.BlockSpec((1, VT_ROWS, t), lambda i, j: (i, 0, j))],
        out_shape=[jax.ShapeDtypeStruct((b, MLA_HEADS, s, 2 * LANE), BF16),
                   jax.ShapeDtypeStruct((b, s, 2 * LANE), BF16),
                   jax.ShapeDtypeStruct((b, VT_ROWS, s), BF16)],
        compiler_params=_cparams(("arbitrary", "arbitrary")),
        name="mla_prep",
    )(proj, proj, proj, proj, cos_m, sin_m, qn_row, kvn_row, wq_all, wrot)


def _mla_attn_kernel(q_ref, k_ref, vt_ref, z_ref, wuv_ref, o_ref, m_ref, acc_ref,
                     s0_ref, s1_ref, mt0_ref, mt1_ref):
    tq = q_ref.shape[2]
    t = ATT_T
    n_tiles = pl.program_id(1) + 1

    m_ref[...] = jnp.full_like(m_ref, -jnp.inf)
    acc_ref[...] = jnp.zeros_like(acc_ref)

    def key_tile(kt):
        k0 = pl.multiple_of(jnp.minimum(kt, n_tiles - 1) * t, t)
        return k_ref[0, pl.ds(k0, t), :], vt_ref[0, :, pl.ds(k0, t)]

    def scores(h, k, s_ref, mt_ref):
        cols = slice(h * tq, (h + 1) * tq)
        s = _dot_nt(k, q_ref[0, h])
        s_ref[:, cols] = s
        mt_ref[:, cols] = jnp.max(s, axis=0, keepdims=True)

    def accumulate(h, vt, s_ref, mt_ref):
        cols = slice(h * tq, (h + 1) * tq)
        m_prev = m_ref[:, cols]
        m_new = jnp.maximum(m_prev, mt_ref[:, cols])
        alpha = jnp.exp2(m_prev - m_new)
        p = jnp.exp2(s_ref[:, cols] - m_new).astype(BF16)
        acc_ref[:, cols] = alpha * acc_ref[:, cols] + _dot(vt, p)
        m_ref[:, cols] = m_new

    def overlapped(k_next, vt_cur, s_next, mt_next, s_cur, mt_cur):
        scores(0, k_next, s_next, mt_next)
        for h in range(MLA_HEADS):
            if h + 1 < MLA_HEADS:
                scores(h + 1, k_next, s_next, mt_next)
            accumulate(h, vt_cur, s_cur, mt_cur)

    k_first, _ = key_tile(0)
    for h in range(MLA_HEADS):
        scores(h, k_first, s0_ref, mt0_ref)

    def pair(j, carry):
        k_a, vt_a = key_tile(2 * j)
        k_b, vt_b = key_tile(2 * j + 1)
        k_c, _ = key_tile(2 * j + 2)
        overlapped(k_b, vt_a, s1_ref, mt1_ref, s0_ref, mt0_ref)
        overlapped(k_c, vt_b, s0_ref, mt0_ref, s1_ref, mt1_ref)
        return carry

    lax.fori_loop(0, n_tiles // 2, pair, 0)

    @pl.when(n_tiles % 2 == 1)
    def _():
        _, vt_last = key_tile(n_tiles - 1)
        for h in range(MLA_HEADS):
            accumulate(h, vt_last, s0_ref, mt0_ref)

    outs = []
    for h in range(MLA_HEADS):
        cols = slice(h * tq, (h + 1) * tq)
        o_h = (acc_ref[0:LANE, cols] / acc_ref[LANE:LANE + 1, cols]).astype(BF16)
        outs.append(_dot(wuv_ref[h], o_h))
    o_t = jnp.concatenate(outs, axis=0)
    z = z_ref[0].astype(F32)
    o_ref[0] = (o_t.T * _silu(z)).astype(BF16)


def _mla_attn(q_s, k_c, v_t, proj, wuv):
    b, h, s, _ = q_s.shape
    t = ATT_T
    nq = h * t
    vrows = v_t.shape[1]
    return pl.pallas_call(
        _mla_attn_kernel,
        grid=(b, s // t),
        in_specs=[pl.BlockSpec((1, h, t, 2 * LANE), lambda i, j: (i, 0, j, 0)),
                  pl.BlockSpec((1, s, 2 * LANE), lambda i, j: (i, 0, 0)),
                  pl.BlockSpec((1, vrows, s), lambda i, j: (i, 0, 0)),
                  pl.BlockSpec((1, t, MLA_WIDTH), lambda i, j: (i, j, OFF_MZ // MLA_WIDTH)),
                  pl.BlockSpec(wuv.shape, lambda i, j: (0, 0, 0))],
        out_specs=pl.BlockSpec((1, t, MLA_WIDTH), lambda i, j: (i, j, 0)),
        out_shape=jax.ShapeDtypeStruct((b, s, MLA_WIDTH), BF16),
        scratch_shapes=[pltpu.VMEM((1, nq), F32), pltpu.VMEM((vrows, nq), F32),
                        pltpu.VMEM((t, nq), F32), pltpu.VMEM((t, nq), F32),
                        pltpu.VMEM((1, nq), F32), pltpu.VMEM((1, nq), F32)],
        compiler_params=_cparams(("arbitrary", "arbitrary")),
        name="mla_attn",
    )(q_s, k_c, v_t, proj, wuv)


def _out_proj_kernel(r_ref, m_ref, g_ref, x_ref, mod_ref, w_ref, fw_ref, o_ref, *, final):
    y = _dot(r_ref[0], w_ref[0:RET_WIDTH, :])
    y += _dot(m_ref[0], w_ref[RET_WIDTH:RET_WIDTH + MLA_WIDTH, :])
    y += _dot(g_ref[0], w_ref[RET_WIDTH + MLA_WIDTH:, :])
    gate = mod_ref[0, 2:3, :]
    x = x_ref[0] + gate * y
    if final:
        x = x * lax.rsqrt(jnp.mean(x * x, axis=-1, keepdims=True) + EPS) * fw_ref[...]
    o_ref[0] = x


def _out_proj(r_o, m_o, g_o, x, mod3, w_out_bf, final_row, final, t):
    b, s, d = x.shape

    def tok(w):
        return pl.BlockSpec((1, t, w), lambda i, j: (i, j, 0))

    return pl.pallas_call(
        functools.partial(_out_proj_kernel, final=final),
        grid=(b, s // t),
        in_specs=[tok(RET_WIDTH), tok(MLA_WIDTH), tok(GLA_WIDTH), tok(d),
                  pl.BlockSpec((1, 3, d), lambda i, j: (i, 0, 0)),
                  pl.BlockSpec(w_out_bf.shape, lambda i, j: (0, 0)),
                  pl.BlockSpec((1, d), lambda i, j: (0, 0))],
        out_specs=tok(d),
        out_shape=jax.ShapeDtypeStruct((b, s, d), F32),
        compiler_params=_cparams(("arbitrary", "arbitrary")),
        name="out_proj",
    )(r_o, m_o, g_o, x, mod3, w_out_bf, final_row)


def _pack_w_in(w_in_l):
    src = np.full((NP_COLS,), -1, np.int64)
    sign = np.ones((NP_COLS,), np.float32)
    ret0, mla0, gla0 = 0, 4 * RET_WIDTH, 4 * RET_WIDTH + (MLA_Q_RANK + MLA_KV_RANK + MLA_ROPE + MLA_WIDTH)
    half = RET_HEAD_DIM // 2
    for p in range(LANE):
        hh, i = p // half, p % half
        for base, off in ((ret0, OFF_RQ), (ret0 + RET_WIDTH, OFF_RK)):
            src[off + p] = base + hh * RET_HEAD_DIM + i
            src[off + LANE + p] = base + hh * RET_HEAD_DIM + half + i
    src[OFF_RV:OFF_RV + RET_WIDTH] = ret0 + 2 * RET_WIDTH + np.arange(RET_WIDTH)
    src[OFF_RZ:OFF_RZ + RET_WIDTH] = ret0 + 3 * RET_WIDTH + np.arange(RET_WIDTH)
    src[OFF_MQ:OFF_MQ + MLA_Q_RANK] = mla0 + np.arange(MLA_Q_RANK)
    src[OFF_KV:OFF_KV + MLA_KV_RANK] = mla0 + MLA_Q_RANK + np.arange(MLA_KV_RANK)
    kr0 = mla0 + MLA_Q_RANK + MLA_KV_RANK
    src[OFF_MB:OFF_MB + MLA_ROPE] = kr0 + np.arange(MLA_ROPE)
    hm = MLA_ROPE // 2
    src[OFF_MC:OFF_MC + hm] = kr0 + hm + np.arange(hm)
    sign[OFF_MC:OFF_MC + hm] = -1.0
    src[OFF_MC + hm:OFF_MC + MLA_ROPE] = kr0 + np.arange(hm)
    src[OFF_MZ:OFF_MZ + MLA_WIDTH] = kr0 + MLA_ROPE + np.arange(MLA_WIDTH)
    src[OFF_GQ:OFF_GQ + GLA_KWIDTH] = gla0 + np.arange(GLA_KWIDTH)
    src[OFF_GK:OFF_GK + GLA_KWIDTH] = gla0 + GLA_KWIDTH + np.arange(GLA_KWIDTH)
    src[OFF_GV:OFF_GV + GLA_WIDTH] = gla0 + 2 * GLA_KWIDTH + np.arange(GLA_WIDTH)
    gg0 = gla0 + 2 * GLA_KWIDTH + GLA_WIDTH
    src[OFF_MB + MLA_ROPE:OFF_MB + MLA_ROPE + GLA_GATE_RANK] = gg0 + np.arange(GLA_GATE_RANK)
    src[OFF_GZ:OFF_GZ + GLA_WIDTH] = gg0 + GLA_GATE_RANK + np.arange(GLA_WIDTH)
    valid = src >= 0
    gathered = jnp.take(w_in_l, jnp.asarray(np.where(valid, src, 0)), axis=1)
    packed = gathered * jnp.asarray(np.where(valid, sign, 0.0))[None, :]
    return packed.astype(BF16)


def _pack_mla_q(w_uq_l, w_abs_l):
    r = w_uq_l.shape[0]
    hd = MLA_NOPE + MLA_ROPE
    hm = MLA_ROPE // 2
    w3 = w_uq_l.reshape(r, MLA_HEADS, hd)
    pe = w3[:, :, MLA_NOPE:]
    pad = jnp.zeros((r, MLA_HEADS, LANE - MLA_ROPE), F32)
    main = jnp.concatenate([jnp.moveaxis(w_abs_l, 0, 1), pe, pad], axis=-1)
    rot = jnp.concatenate([-pe[:, :, hm:], pe[:, :, :hm], pad], axis=-1)
    return (main.reshape(r, MLA_HEADS * 2 * LANE).astype(BF16),
            rot.reshape(r, MLA_HEADS * LANE).astype(BF16))


def kernel(x, c, positions, norm_w, ada_w, ada_b, w_in, mla_q_norm, w_uq, mla_kv_norm, w_ukv,
           gla_w_g2, gla_b_g2, gla_norm, w_out, final_norm):
    b, s, d = x.shape
    depth = w_in.shape[0]
    t_tok = min(512, s)

    mod = _ada_mod(c, ada_w, ada_b).reshape(depth, b, 3, d)
    cos_r, sin_r, cos_m, sin_m = _rope_tables(positions)

    kv_hd = MLA_NOPE + MLA_V
    q_hd = MLA_NOPE + MLA_ROPE
    w_ukv4 = w_ukv.reshape(depth, MLA_KV_RANK, MLA_HEADS, kv_hd)
    wk_nope = jnp.moveaxis(w_ukv4[..., :MLA_NOPE], 2, 1)
    wuv = jnp.transpose(w_ukv4[..., MLA_NOPE:], (0, 2, 3, 1)).astype(BF16)
    wq_nope = jnp.moveaxis(
        w_uq.reshape(depth, MLA_Q_RANK, MLA_HEADS, q_hd)[..., :MLA_NOPE], 2, 1)
    w_abs = _absorb_weights(wq_nope, wk_nope)

    final_row = final_norm.reshape(1, d)
    for l in range(depth):
        w_packed = _pack_w_in(w_in[l])
        wq_all, wrot = _pack_mla_q(w_uq[l], w_abs[l])
        w_g2p = jnp.zeros((LANE, GLA_KWIDTH), F32).at[MLA_ROPE:MLA_ROPE + GLA_GATE_RANK].set(
            gla_w_g2[l]).astype(BF16)

        proj = _in_proj(x, mod[l], norm_w[l].reshape(1, d), w_packed, t_tok)
        r_o = _retention(proj, cos_r, sin_r)
        g_o = _gla(proj, w_g2p, gla_b_g2[l].reshape(1, GLA_KWIDTH),
                   jnp.tile(gla_norm[l], GLA_HEADS).reshape(1, GLA_WIDTH))
        q_s, k_c, v_t = _mla_prep(proj, cos_m, sin_m, mla_q_norm[l].reshape(1, MLA_Q_RANK),
                                  mla_kv_norm[l].reshape(1, MLA_KV_RANK), wq_all, wrot)
        m_o = _mla_attn(q_s, k_c, v_t, proj, wuv[l])
        x = _out_proj(r_o, m_o, g_o, x, mod[l], w_out[l].astype(BF16), final_row,
                      l == depth - 1, t_tok)
    return x
```

```python
import functools

import numpy as np

import jax
import jax.numpy as jnp
from jax import lax
from jax.experimental import pallas as pl
from jax.experimental.pallas import tpu as pltpu

F32 = jnp.float32
BF16 = jnp.bfloat16

D_MODEL = 1024
CHUNK = 64
EPS = 1e-6
ROPE_THETA = 10000.0

RET_HEADS = 4
RET_HEAD_DIM = 64
RET_WIDTH = 256
MLA_HEADS = 8
MLA_NOPE = 64
MLA_ROPE = 32
MLA_V = 64
MLA_WIDTH = 512
MLA_Q_RANK = 256
MLA_KV_RANK = 128
GLA_HEADS = 4
GLA_DK = 32
GLA_DV = 64
GLA_KWIDTH = 128
GLA_WIDTH = 256
GLA_GATE_RANK = 16
GLA_TAU = 16.0
IN_COLS = 2736

LANE = 128

OFF_MZ = 0
OFF_RQ = 512
OFF_RK = 768
OFF_RV = 1024
OFF_RZ = 1280
OFF_MQ = 1536
OFF_GV = 1792
OFF_GZ = 2048
OFF_KV = 2304
OFF_MB = 2432
MB_ROT_LANE = 48
OFF_GQ = 2560
OFF_GK = 2688
NP_COLS = 2816

RET_TILE = 256
RET_SEQS = 4
GLA_TILE = 256
GLA_SEQS = 4
ATT_T = 256
ATT_SEQS = 2
VT_ROWS = 144
LOG2E = 1.4426950408889634
MASK_NEG = -1e30
VMEM_LIMIT = 56 * 1024 * 1024


def _cparams(sem, flags=None):
    return pltpu.CompilerParams(dimension_semantics=sem, vmem_limit_bytes=VMEM_LIMIT, flags=flags)


def _dot(a, b):
    return jnp.dot(a, b, preferred_element_type=F32)


def _dot_nt(a, b):
    return lax.dot_general(a, b, (((1,), (1,)), ((), ())), preferred_element_type=F32)


def _dot_tn(a, b):
    return lax.dot_general(a, b, (((0,), (0,)), ((), ())), preferred_element_type=F32)


def _silu(x):
    return x / (1.0 + jnp.exp(-x))


def _ada_kernel(c_ref, w_ref, b_ref, o_ref):
    c = c_ref[...]
    o_ref[0] = _dot(_silu(c), w_ref[0]) + b_ref[0]


def _ada_mod(c, ada_w, ada_b):
    depth, d, d3 = ada_w.shape
    b = c.shape[0]
    nblk = d3 // d
    return pl.pallas_call(
        _ada_kernel,
        grid=(depth, nblk),
        in_specs=[
            pl.BlockSpec((b, d), lambda l, j: (0, 0)),
            pl.BlockSpec((1, d, d), lambda l, j: (l, 0, j)),
            pl.BlockSpec((1, 1, d), lambda l, j: (l, 0, j)),
        ],
        out_specs=pl.BlockSpec((1, b, d), lambda l, j: (l, 0, j)),
        out_shape=jax.ShapeDtypeStruct((depth, b, d3), F32),
        compiler_params=_cparams(("arbitrary", "arbitrary")),
        name="ada_mod",
    )(c, ada_w, ada_b.reshape(depth, 1, d3))


def _rope_table_kernel(pos_ref, inv_ref, cr_ref, sr_ref, cm_ref, sm_ref):
    half_r = RET_HEAD_DIM // 2
    half_m = MLA_ROPE // 2
    pos = pos_ref[0].astype(F32)
    ang = pos * inv_ref[...]
    lane = lax.broadcasted_iota(jnp.int32, ang.shape, 1)
    is_r = lane < half_r
    is_m = (lane >= half_r) & (lane < half_r + half_m)

    def tables(x):
        r = jnp.where(is_r, x, 0.0)
        r = r + pltpu.roll(r, half_r, 1)
        r = r + pltpu.roll(r, 2 * half_r, 1)
        m = pltpu.roll(jnp.where(is_m, x, 0.0), LANE - half_r, 1)
        m = m + pltpu.roll(m, half_m, 1)
        return r, m

    cr_ref[0], cm_ref[0] = tables(jnp.cos(ang))
    sr_ref[0], sm_ref[0] = tables(jnp.sin(ang))


def _rope_tables(positions):
    b, s = positions.shape
    t = min(s, 512)
    half_r = RET_HEAD_DIM // 2
    half_m = MLA_ROPE // 2
    inv_r = ROPE_THETA ** (-jnp.arange(half_r, dtype=F32) / half_r)
    inv_m = ROPE_THETA ** (-jnp.arange(half_m, dtype=F32) / half_m)
    inv_row = jnp.concatenate(
        [inv_r, inv_m, jnp.zeros((LANE - half_r - half_m,), F32)]).reshape(1, LANE)
    tab = pl.BlockSpec((1, t, LANE), lambda i, j: (i, j, 0))
    shp = jax.ShapeDtypeStruct((b, s, LANE), F32)
    return pl.pallas_call(
        _rope_table_kernel,
        grid=(b, s // t),
        in_specs=[pl.BlockSpec((1, t, 1), lambda i, j: (i, j, 0)),
                  pl.BlockSpec((1, LANE), lambda i, j: (0, 0))],
        out_specs=[tab, tab, tab, tab],
        out_shape=[shp, shp, shp, shp],
        compiler_params=_cparams(("arbitrary", "arbitrary")),
        name="rope_tables",
    )(positions.reshape(b, s, 1), inv_row)


IN_NCHUNK = 512


def _in_proj_kernel(x_ref, mod_ref, nw_ref, w_ref, o_ref):
    x = x_ref[0]
    shift = mod_ref[0, 0:1, :]
    scale = mod_ref[0, 1:2, :]
    y = x * lax.rsqrt(jnp.mean(x * x, axis=-1, keepdims=True) + EPS)
    y = y * nw_ref[...]
    h = (y * (1.0 + scale) + shift).astype(BF16)
    ncols = o_ref.shape[-1]
    for c0 in range(0, ncols, IN_NCHUNK):
        c1 = min(c0 + IN_NCHUNK, ncols)
        o_ref[0, :, c0:c1] = _dot(h, w_ref[:, c0:c1]).astype(BF16)


def _in_proj(x, mod3, norm_w_row, w_packed, t):
    b, s, d = x.shape
    ncols = w_packed.shape[1]
    return pl.pallas_call(
        _in_proj_kernel,
        grid=(b, s // t),
        in_specs=[
            pl.BlockSpec((1, t, d), lambda i, j: (i, j, 0)),
            pl.BlockSpec((1, 3, d), lambda i, j: (i, 0, 0)),
            pl.BlockSpec((1, d), lambda i, j: (0, 0)),
            pl.BlockSpec((d, ncols), lambda i, j: (0, 0)),
        ],
        out_specs=pl.BlockSpec((1, t, ncols), lambda i, j: (i, j, 0)),
        out_shape=jax.ShapeDtypeStruct((b, s, ncols), BF16),
        compiler_params=_cparams(("arbitrary", "arbitrary")),
        name="in_proj",
    )(x, mod3, norm_w_row, w_packed)


def _retention_kernel(q_ref, k_ref, v_ref, z_ref, cos_ref, sin_ref, dmat_ref, qw_ref, kw_ref,
                      sdec_ref, bd_ref, ones_ref, o_ref, state_ref):
    nb = q_ref.shape[0]

    @pl.when(pl.program_id(1) == 0)
    def _():
        state_ref[...] = jnp.zeros_like(state_ref)

    def rope(ref, i):
        a = ref[i].astype(F32)
        cos, sin = cos_ref[i], sin_ref[i]
        x1 = a[:, :LANE]
        x2 = a[:, LANE:]
        return jnp.concatenate([x1 * cos - x2 * sin, x2 * cos + x1 * sin], axis=-1)

    lane = lax.broadcasted_iota(jnp.int32, (1, RET_WIDTH), 1)
    qk_head = (lane % LANE) // (RET_HEAD_DIM // 2)
    v_head = lane // RET_HEAD_DIM
    zero_bf = jnp.zeros((), BF16)

    def stack(a, head_of_lane):
        return jnp.concatenate(
            [jnp.where(head_of_lane == h, a, zero_bf) for h in range(RET_HEADS)], axis=0)

    seqs = range(nb)
    q = [rope(q_ref, i) for i in seqs]
    k = [rope(k_ref, i) * (RET_HEAD_DIM ** -0.5) for i in seqs]
    v = [v_ref[i] for i in seqs]
    state = [state_ref[i] for i in seqs]
    u = [_dot_tn((k[i] * kw_ref[...]).astype(BF16), v[i]) for i in seqs]
    inter = [_dot((q[i] * qw_ref[...]).astype(BF16), state[i].astype(BF16)) for i in seqs]
    scores = [_dot_nt(q[i].astype(BF16), stack(k[i].astype(BF16), qk_head)) for i in seqs]
    for i in seqs:
        state_ref[i] = state[i] * sdec_ref[...] + u[i] * bd_ref[...]
    intra = [_dot((scores[i] * dmat_ref[...]).astype(BF16), stack(v[i], v_head)) for i in seqs]
    o = [intra[i] + inter[i] for i in seqs]
    ms = [_dot((o[i] * o[i]).astype(BF16), ones_ref[...]) for i in seqs]
    for i in seqs:
        y = o[i] * lax.rsqrt(ms[i] + EPS)
        z = z_ref[i].astype(F32)
        o_ref[i] = (y * _silu(z)).astype(BF16)


def _retention_consts(t):
    f32 = F32
    h = RET_HEADS
    log_gamma = jnp.log1p(-jnp.exp2(-5.0 - jnp.arange(h, dtype=f32)))
    idx = jnp.arange(t, dtype=f32)
    dist = jnp.abs(idx[:, None] - idx[None, :])
    ci = jnp.arange(t) // CHUNK
    vis = (ci[None, :] <= ci[:, None]).astype(f32)
    dmat = jnp.exp(log_gamma[:, None, None] * dist[None]) * vis[None]
    dmat = jnp.moveaxis(dmat, 0, 1).reshape(t, h * t)
    lane = jnp.arange(RET_WIDTH)
    qk_head = (lane % LANE) // (RET_HEAD_DIM // 2)
    v_head = lane // RET_HEAD_DIM
    lg_lane = log_gamma[qk_head]
    qw = jnp.exp((idx + 1.0)[:, None] * lg_lane[None, :])
    kw = jnp.exp((t - 1.0 - idx)[:, None] * lg_lane[None, :])
    bd = (qk_head[:, None] == v_head[None, :]).astype(f32)
    sdec = jnp.exp(t * lg_lane)[:, None] * bd
    ones = ((v_head[:, None] == v_head[None, :]).astype(f32) / RET_HEAD_DIM).astype(BF16)
    return dmat, qw, kw, sdec, bd, ones


def _retention(proj, cos_r, sin_r):
    b, s, _ = proj.shape
    t = min(RET_TILE, s)
    dmat, qw, kw, sdec, bd, ones = _retention_consts(t)
    w = RET_WIDTH

    nb = RET_SEQS if b % RET_SEQS == 0 else 1

    def col(off):
        return pl.BlockSpec((nb, t, w), lambda i, j, o=off // w: (i, j, o))

    def const(a):
        return pl.BlockSpec(a.shape, lambda i, j: (0, 0))

    tab = pl.BlockSpec((nb, t, LANE), lambda i, j: (i, j, 0))
    return pl.pallas_call(
        _retention_kernel,
        grid=(b // nb, s // t),
        in_specs=[col(OFF_RQ), col(OFF_RK), col(OFF_RV), col(OFF_RZ), tab, tab,
                  const(dmat), const(qw), const(kw), const(sdec), const(bd), const(ones)],
        out_specs=pl.BlockSpec((nb, t, w), lambda i, j: (i, j, 0)),
        out_shape=jax.ShapeDtypeStruct((b, s, w), BF16),
        scratch_shapes=[pltpu.VMEM((nb, w, w), F32)],
        compiler_params=_cparams(("arbitrary", "arbitrary")),
        name="retention",
    )(proj, proj, proj, proj, cos_r, sin_r, dmat, qw, kw, sdec, bd, ones)


def _gla_kernel(q_ref, k_ref, v_ref, z_ref, mb_ref, wg_ref, bg_ref, ltri_ref, bdt_ref, ones_ref,
                nw_ref, o_ref, state_ref):
    nb, t = q_ref.shape[0], q_ref.shape[1]
    nchunk = t // CHUNK

    @pl.when(pl.program_id(1) == 0)
    def _():
        state_ref[...] = jnp.zeros_like(state_ref)

    lane_k = lax.broadcasted_iota(jnp.int32, (1, GLA_KWIDTH), 1) // GLA_DK
    lane_v = lax.broadcasted_iota(jnp.int32, (1, GLA_WIDTH), 1) // GLA_DV
    row_i = lax.broadcasted_iota(jnp.int32, (CHUNK, GLA_HEADS * CHUNK), 0)
    col_j = lax.broadcasted_iota(jnp.int32, (CHUNK, GLA_HEADS * CHUNK), 1) % CHUNK
    causal = row_i >= col_j
    zero_bf = jnp.zeros((), BF16)
    ltri = ltri_ref[...]
    bdt = bdt_ref[...]
    ones = ones_ref[...]
    nw = nw_ref[...]

    def stack(a, head_of_lane):
        return jnp.concatenate(
            [jnp.where(head_of_lane == h, a, zero_bf) for h in range(GLA_HEADS)], axis=0)

    seqs = range(nb)
    g = [_dot(mb_ref[i], wg_ref[...]) + bg_ref[...] for i in seqs]
    log_a = [(jnp.minimum(x, 0.0) - jnp.log1p(jnp.exp(-jnp.abs(x)))) / GLA_TAU for x in g]
    la_hi = [x.astype(BF16) for x in log_a]
    la_lo = [(x - h.astype(F32)).astype(BF16) for x, h in zip(log_a, la_hi)]
    cum = [_dot(ltri, h) + _dot(ltri, l) for h, l in zip(la_hi, la_lo)]
    q = [q_ref[i].astype(F32) for i in seqs]
    k = [k_ref[i].astype(F32) * (GLA_DK ** -0.5) for i in seqs]
    e_pos = [jnp.exp(x) for x in cum]
    e_neg = [jnp.exp(-x) for x in cum]
    q_pos = [(a * e).astype(BF16) for a, e in zip(q, e_pos)]
    q_neg = [(a * e).astype(BF16) for a, e in zip(q, e_neg)]
    k_neg = [(a * e).astype(BF16) for a, e in zip(k, e_neg)]
    k_pos = [(a * e).astype(BF16) for a, e in zip(k, e_pos)]

    units = [(i, slice(c * CHUNK, (c + 1) * CHUNK)) for i in seqs for c in range(nchunk)]
    v_c = [v_ref[i, r, :] for i, r in units]
    last = [cum[i][r.stop - 1:r.stop, :] for i, r in units]
    k_st = [(k[i][r] * jnp.exp(l - cum[i][r])).astype(BF16) for (i, r), l in zip(units, last)]
    u_t = [_dot_tn(vc, ks) for vc, ks in zip(v_c, k_st)]
    past = [_dot_nt(q_pos[i][r], stack(k_neg[i][r], lane_k)) for i, r in units]
    fut = [_dot_nt(q_neg[i][r], stack(k_pos[i][r], lane_k)) for i, r in units]
    attn = [jnp.where(causal, p, f).astype(BF16) for p, f in zip(past, fut)]
    intra = [_dot(a, stack(vc, lane_v)) for a, vc in zip(attn, v_c)]
    states = []
    for i in seqs:
        st = state_ref[i]
        for c in range(nchunk):
            states.append(st)
            st = st * jnp.exp(last[i * nchunk + c]) + u_t[i * nchunk + c] * bdt
        state_ref[i] = st
    inter = [_dot_nt(q_pos[i][r], st.astype(BF16)) for (i, r), st in zip(units, states)]
    o = [a + b for a, b in zip(intra, inter)]
    ms = [_dot((x * x).astype(BF16), ones) for x in o]
    for (i, r), x, m in zip(units, o, ms):
        y = x * lax.rsqrt(m + EPS) * nw
        z = z_ref[i, r, :].astype(F32)
        o_ref[i, r, :] = (y * _silu(z)).astype(BF16)


def _gla(proj, w_g2p, b_g2_row, gla_norm_row):
    b, s, _ = proj.shape
    t = min(GLA_TILE, s)
    idx = jnp.arange(t)
    ltri = ((idx[:, None] >= idx[None, :]) & (idx[:, None] // CHUNK == idx[None, :] // CHUNK)).astype(BF16)
    kh = jnp.arange(GLA_KWIDTH) // GLA_DK
    vh = jnp.arange(GLA_WIDTH) // GLA_DV
    bdt = (vh[:, None] == kh[None, :]).astype(F32)
    ones = ((vh[:, None] == vh[None, :]).astype(F32) / GLA_DV).astype(BF16)

    nb = GLA_SEQS if b % GLA_SEQS == 0 else 1

    def col(off, w):
        return pl.BlockSpec((nb, t, w), lambda i, j, o=off // w: (i, j, o))

    def const(a):
        return pl.BlockSpec(a.shape, lambda i, j: (0, 0))

    return pl.pallas_call(
        _gla_kernel,
        grid=(b // nb, s // t),
        in_specs=[col(OFF_GQ, GLA_KWIDTH), col(OFF_GK, GLA_KWIDTH), col(OFF_GV, GLA_WIDTH),
                  col(OFF_GZ, GLA_WIDTH), col(OFF_MB, LANE),
                  const(w_g2p), const(b_g2_row), const(ltri), const(bdt), const(ones),
                  const(gla_norm_row)],
        out_specs=pl.BlockSpec((nb, t, GLA_WIDTH), lambda i, j: (i, j, 0)),
        out_shape=jax.ShapeDtypeStruct((b, s, GLA_WIDTH), BF16),
        scratch_shapes=[pltpu.VMEM((nb, GLA_WIDTH, GLA_KWIDTH), F32)],
        compiler_params=_cparams(("arbitrary", "arbitrary")),
        name="gla",
    )(proj, proj, proj, proj, proj, w_g2p, b_g2_row, ltri, bdt, ones, gla_norm_row)


def _absorb_kernel(wq_ref, wk_ref, o_ref):
    o_ref[0, 0] = lax.dot_general(wq_ref[0, 0], wk_ref[0, 0], (((1,), (1,)), ((), ())),
                                  precision=lax.Precision.HIGHEST, preferred_element_type=F32)


def _absorb_weights(wq_nope, wk_nope):
    depth, h, r, dn = wq_nope.shape
    kr = wk_nope.shape[2]
    return pl.pallas_call(
        _absorb_kernel,
        grid=(depth, h),
        in_specs=[pl.BlockSpec((1, 1, r, dn), lambda l, i: (l, i, 0, 0)),
                  pl.BlockSpec((1, 1, kr, dn), lambda l, i: (l, i, 0, 0))],
        out_specs=pl.BlockSpec((1, 1, r, kr), lambda l, i: (l, i, 0, 0)),
        out_shape=jax.ShapeDtypeStruct((depth, h, r, kr), F32),
        compiler_params=_cparams(("arbitrary", "arbitrary")),
        name="mla_absorb",
    )(wq_nope, wk_nope)


def _mla_prep_kernel(mq_ref, kv_ref, mb_ref, cos_ref, sin_ref, qnw_ref, kvnw_ref,
                     wq_ref, wrot_ref, q_out, k_out, vt_out, *, n_chunks):
    cos = cos_ref[0]
    sin = sin_ref[0]
    scale = (MLA_NOPE + MLA_ROPE) ** -0.5 * LOG2E
    lat = mq_ref[0].astype(F32)
    lat = lat * lax.rsqrt(jnp.mean(lat * lat, axis=-1, keepdims=True) + EPS) * (qnw_ref[...] * scale)
    lat = lat.astype(BF16)
    t = lat.shape[0]
    tok_chunk = (pl.program_id(1) * t + lax.broadcasted_iota(jnp.int32, (t, LANE), 0)) // CHUNK
    code_lane = lax.broadcasted_iota(jnp.int32, (t, LANE), 1) - MLA_ROPE
    q_code = jnp.where((code_lane > tok_chunk) & (code_lane < n_chunks), MASK_NEG, 0.0)
    k_code = jnp.where(code_lane == tok_chunk, 1.0, 0.0)
    rot_all = _dot(lat, wrot_ref[...])
    main = [_dot(lat, wq_ref[:, 2 * LANE * h:2 * LANE * (h + 1)]) for h in range(MLA_HEADS)]
    per_blk = LANE // MLA_ROPE
    for h in range(MLA_HEADS):
        blk = rot_all[:, LANE * (h // per_blk):LANE * (h // per_blk + 1)]
        lane0 = MLA_ROPE * (h % per_blk)
        rot = blk if lane0 == 0 else pltpu.roll(blk, LANE - lane0, 1)
        q_out[0, h, :, :LANE] = main[h][:, :LANE].astype(BF16)
        q_out[0, h, :, LANE:] = (main[h][:, LANE:] * cos + rot * sin + q_code).astype(BF16)
    kv = kv_ref[0].astype(F32)
    kvn = kv * lax.rsqrt(jnp.mean(kv * kv, axis=-1, keepdims=True) + EPS) * kvnw_ref[...]
    kvn_bf = kvn.astype(BF16)
    k_out[0, :, :LANE] = kvn_bf
    mb = mb_ref[0].astype(F32)
    mb_rot = pltpu.roll(mb, LANE - MB_ROT_LANE, 1)
    k_out[0, :, LANE:] = (mb * cos + mb_rot * sin + k_code).astype(BF16)
    vt_out[0, 0:LANE, :] = kvn.T.astype(BF16)
    pad_rows = vt_out.shape[1] - LANE
    row = lax.broadcasted_iota(jnp.int32, (pad_rows, kv.shape[0]), 0)
    vt_out[0, LANE:, :] = jnp.where(row == 0, 1.0, 0.0).astype(BF16)


def _mla_prep(proj, cos_m, sin_m, qn_row, kvn_row, wq_all, wrot):
    b, s, _ = proj.shape
    t = min(512, s)

    def col(off, w):
        return pl.BlockSpec((1, t, w), lambda i, j, o=off // w: (i, j, o))

    def const(a):
        return pl.BlockSpec(a.shape, lambda i, j: (0, 0))

    tab = pl.BlockSpec((1, t, LANE), lambda i, j: (i, j, 0))
    n_chunks = s // CHUNK
    assert MLA_ROPE + n_chunks <= LANE, "chunk-mask code does not fit the spare contraction lanes"
    return pl.pallas_call(
        functools.partial(_mla_prep_kernel, n_chunks=n_chunks),
        grid=(b, s // t),
        in_specs=[col(OFF_MQ, MLA_Q_RANK), col(OFF_KV, LANE), col(OFF_MB, LANE),
                  tab, tab, const(qn_row), const(kvn_row), const(wq_all), const(wrot)],
        out_specs=[pl.BlockSpec((1, MLA_HEADS, t, 2 * LANE), lambda i, j: (i, 0, j, 0)),
                   pl.BlockSpec((1, t, 2 * LANE), lambda i, j: (i, j, 0)),
                   pl.BlockSpec((1, VT_ROWS, t), lambda i, j: (i, 0, j))],
        out_shape=[jax.ShapeDtypeStruct((b, MLA_HEADS, s, 2 * LANE), BF16),
                   jax.ShapeDtypeStruct((b, s, 2 * LANE), BF16),
                   jax.ShapeDtypeStruct((b, VT_ROWS, s), BF16)],
        compiler_params=_cparams(("arbitrary", "arbitrary")),
        name="mla_prep",
    )(proj, proj, proj, cos_m, sin_m, qn_row, kvn_row, wq_all, wrot)


def _mla_attn_kernel(q_ref, k_ref, vt_ref, z_ref, wuv_ref, o_ref, m_ref, acc_ref,
                     s0_ref, s1_ref, mt0_ref, mt1_ref):
    nb, tq = q_ref.shape[0], q_ref.shape[2]
    t = ATT_T
    n_tiles = pl.program_id(1) + 1
    units = [(i, h) for h in range(MLA_HEADS) for i in range(nb)]

    m_ref[...] = jnp.full_like(m_ref, -jnp.inf)
    acc_ref[...] = jnp.zeros_like(acc_ref)

    def key_start(kt):
        return pl.multiple_of(jnp.minimum(kt, n_tiles - 1) * t, t)

    def scores(i, h, k0, s_ref, mt_ref):
        cols = slice(h * tq, (h + 1) * tq)
        s = _dot_nt(k_ref[i, pl.ds(k0, t), :], q_ref[i, h])
        s_ref[i, :, cols] = s
        mt_ref[i, :, cols] = jnp.max(s, axis=0, keepdims=True)

    def accumulate(i, h, k0, s_ref, mt_ref):
        cols = slice(h * tq, (h + 1) * tq)
        m_prev = m_ref[i, :, cols]
        m_new = jnp.maximum(m_prev, mt_ref[i, :, cols])
        alpha = jnp.exp2(m_prev - m_new)
        p = jnp.exp2(s_ref[i, :, cols] - m_new).astype(BF16)
        pv = _dot(vt_ref[i, :, pl.ds(k0, t)], p)
        acc_ref[i, :, cols] = alpha * acc_ref[i, :, cols] + pv
        m_ref[i, :, cols] = m_new

    def overlapped(k_next, k_cur, s_next, mt_next, s_cur, mt_cur):
        for i, h in units:
            accumulate(i, h, k_cur, s_cur, mt_cur)
            scores(i, h, k_next, s_next, mt_next)

    for i, h in units:
        scores(i, h, 0, s0_ref, mt0_ref)

    def pair(j, carry):
        k_a, k_b, k_c = key_start(2 * j), key_start(2 * j + 1), key_start(2 * j + 2)
        overlapped(k_b, k_a, s1_ref, mt1_ref, s0_ref, mt0_ref)
        overlapped(k_c, k_b, s0_ref, mt0_ref, s1_ref, mt1_ref)
        return carry

    lax.fori_loop(0, n_tiles // 2, pair, 0)

    @pl.when(n_tiles % 2 == 1)
    def _():
        for i, h in units:
            accumulate(i, h, key_start(n_tiles - 1), s0_ref, mt0_ref)

    o_h = [(acc_ref[i, 0:LANE, h * tq:(h + 1) * tq]
            / acc_ref[i, LANE:LANE + 1, h * tq:(h + 1) * tq]).astype(BF16) for i, h in units]
    o_t = [_dot(wuv_ref[h], o) for (i, h), o in zip(units, o_h)]
    for i in range(nb):
        o_seq = jnp.concatenate([o for (ii, h), o in zip(units, o_t) if ii == i], axis=0)
        z = z_ref[i].astype(F32)
        o_ref[i] = (o_seq.T * _silu(z)).astype(BF16)


def _mla_attn(q_s, k_c, v_t, proj, wuv):
    b, h, s, _ = q_s.shape
    t = ATT_T
    nq = h * t
    vrows = v_t.shape[1]
    nb = ATT_SEQS if b % ATT_SEQS == 0 else 1
    return pl.pallas_call(
        _mla_attn_kernel,
        grid=(b // nb, s // t),
        in_specs=[pl.BlockSpec((nb, h, t, 2 * LANE), lambda i, j: (i, 0, j, 0)),
                  pl.BlockSpec((nb, s, 2 * LANE), lambda i, j: (i, 0, 0)),
                  pl.BlockSpec((nb, vrows, s), lambda i, j: (i, 0, 0)),
                  pl.BlockSpec((nb, t, MLA_WIDTH), lambda i, j: (i, j, OFF_MZ // MLA_WIDTH)),
                  pl.BlockSpec(wuv.shape, lambda i, j: (0, 0, 0))],
        out_specs=pl.BlockSpec((nb, t, MLA_WIDTH), lambda i, j: (i, j, 0)),
        out_shape=jax.ShapeDtypeStruct((b, s, MLA_WIDTH), BF16),
        scratch_shapes=[pltpu.VMEM((nb, 1, nq), F32), pltpu.VMEM((nb, vrows, nq), F32),
                        pltpu.VMEM((nb, t, nq), F32), pltpu.VMEM((nb, t, nq), F32),
                        pltpu.VMEM((nb, 1, nq), F32), pltpu.VMEM((nb, 1, nq), F32)],
        compiler_params=_cparams(("arbitrary", "arbitrary")),
        name="mla_attn",
    )(q_s, k_c, v_t, proj, wuv)


def _out_proj_kernel(r_ref, m_ref, g_ref, x_ref, mod_ref, w_ref, fw_ref, o_ref, *, final):
    y = _dot(r_ref[0], w_ref[0:RET_WIDTH, :])
    y += _dot(m_ref[0], w_ref[RET_WIDTH:RET_WIDTH + MLA_WIDTH, :])
    y += _dot(g_ref[0], w_ref[RET_WIDTH + MLA_WIDTH:, :])
    gate = mod_ref[0, 2:3, :]
    x = x_ref[0] + gate * y
    if final:
        x = x * lax.rsqrt(jnp.mean(x * x, axis=-1, keepdims=True) + EPS) * fw_ref[...]
    o_ref[0] = x


def _out_proj(r_o, m_o, g_o, x, mod3, w_out_bf, final_row, final, t):
    b, s, d = x.shape

    def tok(w):
        return pl.BlockSpec((1, t, w), lambda i, j: (i, j, 0))

    return pl.pallas_call(
        functools.partial(_out_proj_kernel, final=final),
        grid=(b, s // t),
        in_specs=[tok(RET_WIDTH), tok(MLA_WIDTH), tok(GLA_WIDTH), tok(d),
                  pl.BlockSpec((1, 3, d), lambda i, j: (i, 0, 0)),
                  pl.BlockSpec(w_out_bf.shape, lambda i, j: (0, 0)),
                  pl.BlockSpec((1, d), lambda i, j: (0, 0))],
        out_specs=tok(d),
        out_shape=jax.ShapeDtypeStruct((b, s, d), F32),
        compiler_params=_cparams(("arbitrary", "arbitrary")),
        name="out_proj",
    )(r_o, m_o, g_o, x, mod3, w_out_bf, final_row)


def _pack_w_in(w_in_l):
    d = w_in_l.shape[0]
    cuts = np.cumsum([RET_WIDTH] * 4 + [MLA_Q_RANK, MLA_KV_RANK, MLA_ROPE, MLA_WIDTH]
                     + [GLA_KWIDTH, GLA_KWIDTH, GLA_WIDTH, GLA_GATE_RANK, GLA_WIDTH])[:-1]
    rq, rk, rv, rz, mq, kv, kr, mz, gq, gk, gv, gg, gz = jnp.split(w_in_l, [int(c) for c in cuts], axis=1)

    def halves_first(w):
        w4 = w.reshape(d, RET_HEADS, 2, RET_HEAD_DIM // 2)
        return jnp.swapaxes(w4, 1, 2).reshape(d, RET_WIDTH)

    hm = MLA_ROPE // 2
    kr_rot = jnp.concatenate([-kr[:, hm:], kr[:, :hm]], axis=1)
    mb_pad = jnp.zeros((d, LANE - MB_ROT_LANE - MLA_ROPE), F32)
    assert MB_ROT_LANE == MLA_ROPE + GLA_GATE_RANK
    packed = {
        OFF_MZ: mz, OFF_RQ: halves_first(rq), OFF_RK: halves_first(rk), OFF_RV: rv, OFF_RZ: rz,
        OFF_MQ: mq, OFF_GV: gv, OFF_GZ: gz, OFF_KV: kv,
        OFF_MB: jnp.concatenate([kr, gg, kr_rot, mb_pad], axis=1), OFF_GQ: gq, OFF_GK: gk,
    }
    pieces, pos = [], 0
    for off in sorted(packed):
        assert off == pos, "packed in-projection layout has a gap or overlap"
        pieces.append(packed[off])
        pos += packed[off].shape[1]
    assert pos == NP_COLS
    return jnp.concatenate(pieces, axis=1).astype(BF16)


def _pack_mla_q(w_uq_l, w_abs_l):
    r = w_uq_l.shape[0]
    hd = MLA_NOPE + MLA_ROPE
    hm = MLA_ROPE // 2
    w3 = w_uq_l.reshape(r, MLA_HEADS, hd)
    pe = w3[:, :, MLA_NOPE:]
    pad = jnp.zeros((r, MLA_HEADS, LANE - MLA_ROPE), F32)
    main = jnp.concatenate([jnp.moveaxis(w_abs_l, 0, 1), pe, pad], axis=-1)
    rot = jnp.concatenate([-pe[:, :, hm:], pe[:, :, :hm]], axis=-1)
    return (main.reshape(r, MLA_HEADS * 2 * LANE).astype(BF16),
            rot.reshape(r, MLA_HEADS * MLA_ROPE).astype(BF16))


def kernel(x, c, positions, norm_w, ada_w, ada_b, w_in, mla_q_norm, w_uq, mla_kv_norm, w_ukv,
           gla_w_g2, gla_b_g2, gla_norm, w_out, final_norm):
    b, s, d = x.shape
    depth = w_in.shape[0]
    t_tok = min(512, s)

    mod = _ada_mod(c, ada_w, ada_b).reshape(depth, b, 3, d)
    cos_r, sin_r, cos_m, sin_m = _rope_tables(positions)

    kv_hd = MLA_NOPE + MLA_V
    q_hd = MLA_NOPE + MLA_ROPE
    w_ukv4 = w_ukv.reshape(depth, MLA_KV_RANK, MLA_HEADS, kv_hd)
    wk_nope = jnp.moveaxis(w_ukv4[..., :MLA_NOPE], 2, 1)
    wuv = jnp.transpose(w_ukv4[..., MLA_NOPE:], (0, 2, 3, 1)).astype(BF16)
    wq_nope = jnp.moveaxis(
        w_uq.reshape(depth, MLA_Q_RANK, MLA_HEADS, q_hd)[..., :MLA_NOPE], 2, 1)
    w_abs = _absorb_weights(wq_nope, wk_nope)

    final_row = final_norm.reshape(1, d)
    for l in range(depth):
        w_packed = _pack_w_in(w_in[l])
        wq_all, wrot = _pack_mla_q(w_uq[l], w_abs[l])
        w_g2p = jnp.zeros((LANE, GLA_KWIDTH), F32).at[MLA_ROPE:MLA_ROPE + GLA_GATE_RANK].set(
            gla_w_g2[l]).astype(BF16)

        proj = _in_proj(x, mod[l], norm_w[l].reshape(1, d), w_packed, t_tok)
        r_o = _retention(proj, cos_r, sin_r)
        g_o = _gla(proj, w_g2p, gla_b_g2[l].reshape(1, GLA_KWIDTH),
                   jnp.tile(gla_norm[l], GLA_HEADS).reshape(1, GLA_WIDTH))
        q_s, k_c, v_t = _mla_prep(proj, cos_m, sin_m, mla_q_norm[l].reshape(1, MLA_Q_RANK),
                                  mla_kv_norm[l].reshape(1, MLA_KV_RANK), wq_all, wrot)
        m_o = _mla_attn(q_s, k_c, v_t, proj, wuv[l])
        x = _out_proj(r_o, m_o, g_o, x, mod[l], w_out[l].astype(BF16), final_row,
                      l == depth - 1, t_tok)
    return x
```

```python
import functools

import numpy as np

import jax
import jax.numpy as jnp
from jax import lax
from jax.experimental import pallas as pl
from jax.experimental.pallas import tpu as pltpu

F32 = jnp.float32
BF16 = jnp.bfloat16

D_MODEL = 1024
CHUNK = 64
EPS = 1e-6
ROPE_THETA = 10000.0

RET_HEADS = 4
RET_HEAD_DIM = 64
RET_WIDTH = 256
MLA_HEADS = 8
MLA_NOPE = 64
MLA_ROPE = 32
MLA_V = 64
MLA_WIDTH = 512
MLA_Q_RANK = 256
MLA_KV_RANK = 128
GLA_HEADS = 4
GLA_DK = 32
GLA_DV = 64
GLA_KWIDTH = 128
GLA_WIDTH = 256
GLA_GATE_RANK = 16
GLA_TAU = 16.0
IN_COLS = 2736

LANE = 128

OFF_MZ = 0
OFF_RQ = 512
OFF_RK = 768
OFF_RV = 1024
OFF_RZ = 1280
OFF_MQ = 1536
OFF_GV = 1792
OFF_GZ = 2048
OFF_KV = 2304
OFF_MB = 2432
MB_ROT_LANE = 48
OFF_GQ = 2560
OFF_GK = 2688
NP_COLS = 2816

RET_TILE = 256
RET_SEQS = 4
GLA_TILE = 256
GLA_SEQS = 4
ATT_T = 256
ATT_SEQS = 2
VT_ROWS = 144
LOG2E = 1.4426950408889634
MASK_NEG = -1e30
VMEM_LIMIT = 56 * 1024 * 1024


def _cparams(sem, flags=None):
    return pltpu.CompilerParams(dimension_semantics=sem, vmem_limit_bytes=VMEM_LIMIT, flags=flags)


def _dot(a, b):
    return jnp.dot(a, b, preferred_element_type=F32)


def _dot_nt(a, b):
    return lax.dot_general(a, b, (((1,), (1,)), ((), ())), preferred_element_type=F32)


def _dot_tn(a, b):
    return lax.dot_general(a, b, (((0,), (0,)), ((), ())), preferred_element_type=F32)


def _silu(x):
    return x / (1.0 + jnp.exp(-x))


def _ada_kernel(c_ref, w_ref, b_ref, o_ref):
    c = c_ref[...]
    o_ref[0] = _dot(_silu(c), w_ref[0]) + b_ref[0]


def _ada_mod(c, ada_w, ada_b):
    depth, d, d3 = ada_w.shape
    b = c.shape[0]
    nblk = d3 // d
    return pl.pallas_call(
        _ada_kernel,
        grid=(depth, nblk),
        in_specs=[
            pl.BlockSpec((b, d), lambda l, j: (0, 0)),
            pl.BlockSpec((1, d, d), lambda l, j: (l, 0, j)),
            pl.BlockSpec((1, 1, d), lambda l, j: (l, 0, j)),
        ],
        out_specs=pl.BlockSpec((1, b, d), lambda l, j: (l, 0, j)),
        out_shape=jax.ShapeDtypeStruct((depth, b, d3), F32),
        compiler_params=_cparams(("arbitrary", "arbitrary")),
        name="ada_mod",
    )(c, ada_w, ada_b.reshape(depth, 1, d3))


def _rope_rows(pos, inv_row):
    half_r = RET_HEAD_DIM // 2
    half_m = MLA_ROPE // 2
    ang = pos * inv_row
    lane = lax.broadcasted_iota(jnp.int32, ang.shape, 1)
    is_r = lane < half_r
    is_m = (lane >= half_r) & (lane < half_r + half_m)

    def tables(x):
        r = jnp.where(is_r, x, 0.0)
        r = r + pltpu.roll(r, half_r, 1)
        r = r + pltpu.roll(r, 2 * half_r, 1)
        m = pltpu.roll(jnp.where(is_m, x, 0.0), LANE - half_r, 1)
        m = m + pltpu.roll(m, half_m, 1)
        return r, m

    cos_r, cos_m = tables(jnp.cos(ang))
    sin_r, sin_m = tables(jnp.sin(ang))
    return cos_r, sin_r, cos_m, sin_m


def _rope_inv_row():
    half_r = RET_HEAD_DIM // 2
    half_m = MLA_ROPE // 2
    inv_r = ROPE_THETA ** (-jnp.arange(half_r, dtype=F32) / half_r)
    inv_m = ROPE_THETA ** (-jnp.arange(half_m, dtype=F32) / half_m)
    return jnp.concatenate(
        [inv_r, inv_m, jnp.zeros((LANE - half_r - half_m,), F32)]).reshape(1, LANE)


IN_NCHUNK = 512


def _retention_kernel(q_ref, k_ref, v_ref, z_ref, cos_ref, sin_ref, dmat_ref, qw_ref, kw_ref,
                      sdec_ref, bd_ref, ones_ref, o_ref, state_ref):
    nb = q_ref.shape[0]

    @pl.when(pl.program_id(1) == 0)
    def _():
        state_ref[...] = jnp.zeros_like(state_ref)

    def rope(ref, i):
        a = ref[i].astype(F32)
        cos, sin = cos_ref[i], sin_ref[i]
        x1 = a[:, :LANE]
        x2 = a[:, LANE:]
        return jnp.concatenate([x1 * cos - x2 * sin, x2 * cos + x1 * sin], axis=-1)

    lane = lax.broadcasted_iota(jnp.int32, (1, RET_WIDTH), 1)
    qk_head = (lane % LANE) // (RET_HEAD_DIM // 2)
    v_head = lane // RET_HEAD_DIM
    zero_bf = jnp.zeros((), BF16)

    def stack(a, head_of_lane):
        return jnp.concatenate(
            [jnp.where(head_of_lane == h, a, zero_bf) for h in range(RET_HEADS)], axis=0)

    seqs = range(nb)
    q = [rope(q_ref, i) for i in seqs]
    k = [rope(k_ref, i) * (RET_HEAD_DIM ** -0.5) for i in seqs]
    v = [v_ref[i] for i in seqs]
    state = [state_ref[i] for i in seqs]
    u = [_dot_tn((k[i] * kw_ref[...]).astype(BF16), v[i]) for i in seqs]
    inter = [_dot((q[i] * qw_ref[...]).astype(BF16), state[i].astype(BF16)) for i in seqs]
    scores = [_dot_nt(q[i].astype(BF16), stack(k[i].astype(BF16), qk_head)) for i in seqs]
    for i in seqs:
        state_ref[i] = state[i] * sdec_ref[...] + u[i] * bd_ref[...]
    intra = [_dot((scores[i] * dmat_ref[...]).astype(BF16), stack(v[i], v_head)) for i in seqs]
    o = [intra[i] + inter[i] for i in seqs]
    ms = [_dot((o[i] * o[i]).astype(BF16), ones_ref[...]) for i in seqs]
    for i in seqs:
        y = o[i] * lax.rsqrt(ms[i] + EPS)
        z = z_ref[i].astype(F32)
        o_ref[i] = (y * _silu(z)).astype(BF16)


def _retention_consts(t):
    f32 = F32
    h = RET_HEADS
    log_gamma = jnp.log1p(-jnp.exp2(-5.0 - jnp.arange(h, dtype=f32)))
    idx = jnp.arange(t, dtype=f32)
    dist = jnp.abs(idx[:, None] - idx[None, :])
    ci = jnp.arange(t) // CHUNK
    vis = (ci[None, :] <= ci[:, None]).astype(f32)
    dmat = jnp.exp(log_gamma[:, None, None] * dist[None]) * vis[None]
    dmat = jnp.moveaxis(dmat, 0, 1).reshape(t, h * t)
    lane = jnp.arange(RET_WIDTH)
    qk_head = (lane % LANE) // (RET_HEAD_DIM // 2)
    v_head = lane // RET_HEAD_DIM
    lg_lane = log_gamma[qk_head]
    qw = jnp.exp((idx + 1.0)[:, None] * lg_lane[None, :])
    kw = jnp.exp((t - 1.0 - idx)[:, None] * lg_lane[None, :])
    bd = (qk_head[:, None] == v_head[None, :]).astype(f32)
    sdec = jnp.exp(t * lg_lane)[:, None] * bd
    ones = ((v_head[:, None] == v_head[None, :]).astype(f32) / RET_HEAD_DIM).astype(BF16)
    return dmat, qw, kw, sdec, bd, ones


def _retention(proj, cos_r, sin_r):
    b, s, _ = proj.shape
    t = min(RET_TILE, s)
    dmat, qw, kw, sdec, bd, ones = _retention_consts(t)
    w = RET_WIDTH

    nb = RET_SEQS if b % RET_SEQS == 0 else 1

    def col(off):
        return pl.BlockSpec((nb, t, w), lambda i, j, o=off // w: (i, j, o))

    def const(a):
        return pl.BlockSpec(a.shape, lambda i, j: (0, 0))

    tab = pl.BlockSpec((nb, t, LANE), lambda i, j: (i, j, 0))
    return pl.pallas_call(
        _retention_kernel,
        grid=(b // nb, s // t),
        in_specs=[col(OFF_RQ), col(OFF_RK), col(OFF_RV), col(OFF_RZ), tab, tab,
                  const(dmat), const(qw), const(kw), const(sdec), const(bd), const(ones)],
        out_specs=pl.BlockSpec((nb, t, w), lambda i, j: (i, j, 0)),
        out_shape=jax.ShapeDtypeStruct((b, s, w), BF16),
        scratch_shapes=[pltpu.VMEM((nb, w, w), F32)],
        compiler_params=_cparams(("arbitrary", "arbitrary")),
        name="retention",
    )(proj, proj, proj, proj, cos_r, sin_r, dmat, qw, kw, sdec, bd, ones)


def _gla_kernel(q_ref, k_ref, v_ref, z_ref, mb_ref, wg_ref, bg_ref, ltri_ref, bdt_ref, ones_ref,
                nw_ref, o_ref, state_ref):
    nb, t = q_ref.shape[0], q_ref.shape[1]
    nchunk = t // CHUNK

    @pl.when(pl.program_id(1) == 0)
    def _():
        state_ref[...] = jnp.zeros_like(state_ref)

    lane_k = lax.broadcasted_iota(jnp.int32, (1, GLA_KWIDTH), 1) // GLA_DK
    lane_v = lax.broadcasted_iota(jnp.int32, (1, GLA_WIDTH), 1) // GLA_DV
    row_i = lax.broadcasted_iota(jnp.int32, (CHUNK, GLA_HEADS * CHUNK), 0)
    col_j = lax.broadcasted_iota(jnp.int32, (CHUNK, GLA_HEADS * CHUNK), 1) % CHUNK
    causal = row_i >= col_j
    zero_bf = jnp.zeros((), BF16)
    ltri = ltri_ref[...]
    bdt = bdt_ref[...]
    ones = ones_ref[...]
    nw = nw_ref[...]

    def stack(a, head_of_lane):
        return jnp.concatenate(
            [jnp.where(head_of_lane == h, a, zero_bf) for h in range(GLA_HEADS)], axis=0)

    seqs = range(nb)
    g = [_dot(mb_ref[i], wg_ref[...]) + bg_ref[...] for i in seqs]
    log_a = [(jnp.minimum(x, 0.0) - jnp.log1p(jnp.exp(-jnp.abs(x)))) / GLA_TAU for x in g]
    la_hi = [x.astype(BF16) for x in log_a]
    la_lo = [(x - h.astype(F32)).astype(BF16) for x, h in zip(log_a, la_hi)]
    cum = [_dot(ltri, h) + _dot(ltri, l) for h, l in zip(la_hi, la_lo)]
    q = [q_ref[i].astype(F32) for i in seqs]
    k = [k_ref[i].astype(F32) * (GLA_DK ** -0.5) for i in seqs]
    e_pos = [jnp.exp(x) for x in cum]
    e_neg = [jnp.exp(-x) for x in cum]
    q_pos = [(a * e).astype(BF16) for a, e in zip(q, e_pos)]
    q_neg = [(a * e).astype(BF16) for a, e in zip(q, e_neg)]
    k_neg = [(a * e).astype(BF16) for a, e in zip(k, e_neg)]
    k_pos = [(a * e).astype(BF16) for a, e in zip(k, e_pos)]

    units = [(i, slice(c * CHUNK, (c + 1) * CHUNK)) for i in seqs for c in range(nchunk)]
    v_c = [v_ref[i, r, :] for i, r in units]
    last = [cum[i][r.stop - 1:r.stop, :] for i, r in units]
    k_st = [(k[i][r] * jnp.exp(l - cum[i][r])).astype(BF16) for (i, r), l in zip(units, last)]
    u_t = [_dot_tn(vc, ks) for vc, ks in zip(v_c, k_st)]
    past = [_dot_nt(q_pos[i][r], stack(k_neg[i][r], lane_k)) for i, r in units]
    fut = [_dot_nt(q_neg[i][r], stack(k_pos[i][r], lane_k)) for i, r in units]
    attn = [jnp.where(causal, p, f).astype(BF16) for p, f in zip(past, fut)]
    intra = [_dot(a, stack(vc, lane_v)) for a, vc in zip(attn, v_c)]
    states = []
    for i in seqs:
        st = state_ref[i]
        for c in range(nchunk):
            states.append(st)
            st = st * jnp.exp(last[i * nchunk + c]) + u_t[i * nchunk + c] * bdt
        state_ref[i] = st
    inter = [_dot_nt(q_pos[i][r], st.astype(BF16)) for (i, r), st in zip(units, states)]
    o = [a + b for a, b in zip(intra, inter)]
    ms = [_dot((x * x).astype(BF16), ones) for x in o]
    for (i, r), x, m in zip(units, o, ms):
        y = x * lax.rsqrt(m + EPS) * nw
        z = z_ref[i, r, :].astype(F32)
        o_ref[i, r, :] = (y * _silu(z)).astype(BF16)


def _gla(proj, w_g2p, b_g2_row, gla_norm_row):
    b, s, _ = proj.shape
    t = min(GLA_TILE, s)
    idx = jnp.arange(t)
    ltri = ((idx[:, None] >= idx[None, :]) & (idx[:, None] // CHUNK == idx[None, :] // CHUNK)).astype(BF16)
    kh = jnp.arange(GLA_KWIDTH) // GLA_DK
    vh = jnp.arange(GLA_WIDTH) // GLA_DV
    bdt = (vh[:, None] == kh[None, :]).astype(F32)
    ones = ((vh[:, None] == vh[None, :]).astype(F32) / GLA_DV).astype(BF16)

    nb = GLA_SEQS if b % GLA_SEQS == 0 else 1

    def col(off, w):
        return pl.BlockSpec((nb, t, w), lambda i, j, o=off // w: (i, j, o))

    def const(a):
        return pl.BlockSpec(a.shape, lambda i, j: (0, 0))

    return pl.pallas_call(
        _gla_kernel,
        grid=(b // nb, s // t),
        in_specs=[col(OFF_GQ, GLA_KWIDTH), col(OFF_GK, GLA_KWIDTH), col(OFF_GV, GLA_WIDTH),
                  col(OFF_GZ, GLA_WIDTH), col(OFF_MB, LANE),
                  const(w_g2p), const(b_g2_row), const(ltri), const(bdt), const(ones),
                  const(gla_norm_row)],
        out_specs=pl.BlockSpec((nb, t, GLA_WIDTH), lambda i, j: (i, j, 0)),
        out_shape=jax.ShapeDtypeStruct((b, s, GLA_WIDTH), BF16),
        scratch_shapes=[pltpu.VMEM((nb, GLA_WIDTH, GLA_KWIDTH), F32)],
        compiler_params=_cparams(("arbitrary", "arbitrary")),
        name="gla",
    )(proj, proj, proj, proj, proj, w_g2p, b_g2_row, ltri, bdt, ones, gla_norm_row)


def _absorb_kernel(wq_ref, wk_ref, o_ref):
    o_ref[0, 0] = lax.dot_general(wq_ref[0, 0], wk_ref[0, 0], (((1,), (1,)), ((), ())),
                                  precision=lax.Precision.HIGHEST, preferred_element_type=F32)


def _absorb_weights(wq_nope, wk_nope):
    depth, h, r, dn = wq_nope.shape
    kr = wk_nope.shape[2]
    return pl.pallas_call(
        _absorb_kernel,
        grid=(depth, h),
        in_specs=[pl.BlockSpec((1, 1, r, dn), lambda l, i: (l, i, 0, 0)),
                  pl.BlockSpec((1, 1, kr, dn), lambda l, i: (l, i, 0, 0))],
        out_specs=pl.BlockSpec((1, 1, r, kr), lambda l, i: (l, i, 0, 0)),
        out_shape=jax.ShapeDtypeStruct((depth, h, r, kr), F32),
        compiler_params=_cparams(("arbitrary", "arbitrary")),
        name="mla_absorb",
    )(wq_nope, wk_nope)


def _mla_attn_kernel(q_ref, k_ref, vt_ref, z_ref, wuv_ref, o_ref, m_ref, acc_ref,
                     s0_ref, s1_ref, mt0_ref, mt1_ref):
    nb, tq = q_ref.shape[0], q_ref.shape[2]
    t = ATT_T
    n_tiles = pl.program_id(1) + 1
    units = [(i, h) for h in range(MLA_HEADS) for i in range(nb)]

    m_ref[...] = jnp.full_like(m_ref, -jnp.inf)
    acc_ref[...] = jnp.zeros_like(acc_ref)

    def key_start(kt):
        return pl.multiple_of(jnp.minimum(kt, n_tiles - 1) * t, t)

    def scores(i, h, k0, s_ref, mt_ref):
        cols = slice(h * tq, (h + 1) * tq)
        s = _dot_nt(k_ref[i, pl.ds(k0, t), :], q_ref[i, h])
        s_ref[i, :, cols] = s
        mt_ref[i, :, cols] = jnp.max(s, axis=0, keepdims=True)

    def accumulate(i, h, k0, s_ref, mt_ref):
        cols = slice(h * tq, (h + 1) * tq)
        m_prev = m_ref[i, :, cols]
        m_new = jnp.maximum(m_prev, mt_ref[i, :, cols])
        alpha = jnp.exp2(m_prev - m_new)
        p = jnp.exp2(s_ref[i, :, cols] - m_new).astype(BF16)
        pv = _dot(vt_ref[i, :, pl.ds(k0, t)], p)
        acc_ref[i, :, cols] = alpha * acc_ref[i, :, cols] + pv
        m_ref[i, :, cols] = m_new

    def overlapped(k_next, k_cur, s_next, mt_next, s_cur, mt_cur):
        for i, h in units:
            accumulate(i, h, k_cur, s_cur, mt_cur)
            scores(i, h, k_next, s_next, mt_next)

    for i, h in units:
        scores(i, h, 0, s0_ref, mt0_ref)

    def pair(j, carry):
        k_a, k_b, k_c = key_start(2 * j), key_start(2 * j + 1), key_start(2 * j + 2)
        overlapped(k_b, k_a, s1_ref, mt1_ref, s0_ref, mt0_ref)
        overlapped(k_c, k_b, s0_ref, mt0_ref, s1_ref, mt1_ref)
        return carry

    lax.fori_loop(0, n_tiles // 2, pair, 0)

    @pl.when(n_tiles % 2 == 1)
    def _():
        for i, h in units:
            accumulate(i, h, key_start(n_tiles - 1), s0_ref, mt0_ref)

    o_h = [(acc_ref[i, 0:LANE, h * tq:(h + 1) * tq]
            / acc_ref[i, LANE:LANE + 1, h * tq:(h + 1) * tq]).astype(BF16) for i, h in units]
    o_t = [_dot(wuv_ref[h], o) for (i, h), o in zip(units, o_h)]
    for i in range(nb):
        o_seq = jnp.concatenate([o for (ii, h), o in zip(units, o_t) if ii == i], axis=0)
        z = z_ref[i].astype(F32)
        o_ref[i] = (o_seq.T * _silu(z)).astype(BF16)


def _mla_attn(q_s, k_c, v_t, proj, wuv):
    b, h, s, _ = q_s.shape
    t = ATT_T
    nq = h * t
    vrows = v_t.shape[1]
    nb = ATT_SEQS if b % ATT_SEQS == 0 else 1
    return pl.pallas_call(
        _mla_attn_kernel,
        grid=(b // nb, s // t),
        in_specs=[pl.BlockSpec((nb, h, t, 2 * LANE), lambda i, j: (i, 0, j, 0)),
                  pl.BlockSpec((nb, s, 2 * LANE), lambda i, j: (i, 0, 0)),
                  pl.BlockSpec((nb, vrows, s), lambda i, j: (i, 0, 0)),
                  pl.BlockSpec((nb, t, MLA_WIDTH), lambda i, j: (i, j, OFF_MZ // MLA_WIDTH)),
                  pl.BlockSpec(wuv.shape, lambda i, j: (0, 0, 0))],
        out_specs=pl.BlockSpec((nb, t, MLA_WIDTH), lambda i, j: (i, j, 0)),
        out_shape=jax.ShapeDtypeStruct((b, s, MLA_WIDTH), BF16),
        scratch_shapes=[pltpu.VMEM((nb, 1, nq), F32), pltpu.VMEM((nb, vrows, nq), F32),
                        pltpu.VMEM((nb, t, nq), F32), pltpu.VMEM((nb, t, nq), F32),
                        pltpu.VMEM((nb, 1, nq), F32), pltpu.VMEM((nb, 1, nq), F32)],
        compiler_params=_cparams(("arbitrary", "arbitrary")),
        name="mla_attn",
    )(q_s, k_c, v_t, proj, wuv)


def _layer_in_kernel(*refs, prev, n_chunks):
    if prev:
        (r_ref, m_ref, g_ref, x_ref, modp_ref, wo_ref, mod_ref, nw_ref, w_ref, cos_ref, sin_ref,
         qnw_ref, kvnw_ref, wq_ref, wrot_ref, xo_ref, p_ref, q_out, k_out, vt_out) = refs
        y = _dot(r_ref[0], wo_ref[0:RET_WIDTH, :])
        y += _dot(m_ref[0], wo_ref[RET_WIDTH:RET_WIDTH + MLA_WIDTH, :])
        y += _dot(g_ref[0], wo_ref[RET_WIDTH + MLA_WIDTH:, :])
        x = x_ref[0] + modp_ref[0, 2:3, :] * y
        xo_ref[0] = x
        cos = cos_ref[0]
        sin = sin_ref[0]
    else:
        (x_ref, pos_ref, inv_ref, mod_ref, nw_ref, w_ref, qnw_ref, kvnw_ref, wq_ref, wrot_ref,
         p_ref, q_out, k_out, vt_out, cr_out, sr_out, cm_out, sm_out) = refs
        x = x_ref[0]
        cr_out[0], sr_out[0], cos, sin = _rope_rows(pos_ref[0].astype(F32), inv_ref[...])
        cm_out[0] = cos
        sm_out[0] = sin
    t = x.shape[0]
    shift = mod_ref[0, 0:1, :]
    scale = mod_ref[0, 1:2, :]
    hn = x * lax.rsqrt(jnp.mean(x * x, axis=-1, keepdims=True) + EPS) * nw_ref[...]
    act = (hn * (1.0 + scale) + shift).astype(BF16)

    def project(c0):
        c1 = min(c0 + IN_NCHUNK, NP_COLS)
        a = _dot(act, w_ref[:, c0:c1])
        p_ref[0, :, c0:c1] = a.astype(BF16)
        return a

    c_mq = OFF_MQ // IN_NCHUNK * IN_NCHUNK
    c_kv = OFF_KV // IN_NCHUNK * IN_NCHUNK
    assert c_kv == OFF_MB // IN_NCHUNK * IN_NCHUNK and c_mq != c_kv
    slab_q = project(c_mq)
    slab_kv = project(c_kv)
    rest = [c0 for c0 in range(0, NP_COLS, IN_NCHUNK) if c0 not in (c_mq, c_kv)]

    sm_scale = (MLA_NOPE + MLA_ROPE) ** -0.5 * LOG2E
    lat = slab_q[:, OFF_MQ - c_mq:OFF_MQ - c_mq + MLA_Q_RANK]
    lat = lat * lax.rsqrt(jnp.mean(lat * lat, axis=-1, keepdims=True) + EPS) * (qnw_ref[...] * sm_scale)
    lat = lat.astype(BF16)
    tok_chunk = (pl.program_id(1) * t + lax.broadcasted_iota(jnp.int32, (t, LANE), 0)) // CHUNK
    code_lane = lax.broadcasted_iota(jnp.int32, (t, LANE), 1) - MLA_ROPE
    q_code = jnp.where((code_lane > tok_chunk) & (code_lane < n_chunks), MASK_NEG, 0.0)
    k_code = jnp.where(code_lane == tok_chunk, 1.0, 0.0)

    kv = slab_kv[:, OFF_KV - c_kv:OFF_KV - c_kv + MLA_KV_RANK]
    kvn = kv * lax.rsqrt(jnp.mean(kv * kv, axis=-1, keepdims=True) + EPS) * kvnw_ref[...]
    k_out[0, :, :LANE] = kvn.astype(BF16)
    mb = slab_kv[:, OFF_MB - c_kv:OFF_MB - c_kv + LANE]
    mb_rot = pltpu.roll(mb, LANE - MB_ROT_LANE, 1)
    k_out[0, :, LANE:] = (mb * cos + mb_rot * sin + k_code).astype(BF16)
    vt_out[0, 0:LANE, :] = kvn.T.astype(BF16)
    pad_rows = vt_out.shape[1] - LANE
    row = lax.broadcasted_iota(jnp.int32, (pad_rows, t), 0)
    vt_out[0, LANE:, :] = jnp.where(row == 0, 1.0, 0.0).astype(BF16)

    rot_all = _dot(lat, wrot_ref[...])
    per_blk = LANE // MLA_ROPE
    heads_per_slab = -(-MLA_HEADS // max(len(rest), 1))
    for h in range(MLA_HEADS):
        if h % heads_per_slab == 0 and rest:
            project(rest.pop(0))
        main = _dot(lat, wq_ref[:, 2 * LANE * h:2 * LANE * (h + 1)])
        blk = rot_all[:, LANE * (h // per_blk):LANE * (h // per_blk + 1)]
        lane0 = MLA_ROPE * (h % per_blk)
        rot = blk if lane0 == 0 else pltpu.roll(blk, LANE - lane0, 1)
        q_out[0, h, :, :LANE] = main[:, :LANE].astype(BF16)
        q_out[0, h, :, LANE:] = (main[:, LANE:] * cos + rot * sin + q_code).astype(BF16)
    for c0 in rest:
        project(c0)


def _layer_in(x, mod3, norm_w_row, w_packed, qn_row, kvn_row, wq_all, wrot, t, *,
              positions=None, tables=None, prev=None):
    b, s, d = x.shape
    n_chunks = s // CHUNK
    assert MLA_ROPE + n_chunks <= LANE, "chunk-mask code does not fit the spare contraction lanes"
    first = prev is None
    assert first == (positions is not None) and first == (tables is None)

    def tok(w):
        return pl.BlockSpec((1, t, w), lambda i, j: (i, j, 0))

    def const(a):
        return pl.BlockSpec(a.shape, lambda i, j: (0,) * a.ndim)

    mod_spec = pl.BlockSpec((1, 3, d), lambda i, j: (i, 0, 0))
    common_specs = [mod_spec, const(norm_w_row), const(w_packed)]
    common_args = [mod3, norm_w_row, w_packed]
    tail_specs = [const(qn_row), const(kvn_row), const(wq_all), const(wrot)]
    tail_args = [qn_row, kvn_row, wq_all, wrot]
    out_specs = [tok(NP_COLS),
                 pl.BlockSpec((1, MLA_HEADS, t, 2 * LANE), lambda i, j: (i, 0, j, 0)),
                 tok(2 * LANE),
                 pl.BlockSpec((1, VT_ROWS, t), lambda i, j: (i, 0, j))]
    out_shape = [jax.ShapeDtypeStruct((b, s, NP_COLS), BF16),
                 jax.ShapeDtypeStruct((b, MLA_HEADS, s, 2 * LANE), BF16),
                 jax.ShapeDtypeStruct((b, s, 2 * LANE), BF16),
                 jax.ShapeDtypeStruct((b, VT_ROWS, s), BF16)]
    if first:
        inv_row = _rope_inv_row()
        in_specs = [tok(d), tok(1), const(inv_row)] + common_specs + tail_specs
        args = [x, positions.reshape(b, s, 1), inv_row] + common_args + tail_args
        out_specs = out_specs + [tok(LANE)] * 4
        out_shape = out_shape + [jax.ShapeDtypeStruct((b, s, LANE), F32)] * 4
    else:
        r_o, m_o, g_o, mod3_prev, w_out_bf = prev
        in_specs = ([tok(RET_WIDTH), tok(MLA_WIDTH), tok(GLA_WIDTH), tok(d), mod_spec, const(w_out_bf)]
                    + common_specs + [tok(LANE), tok(LANE)] + tail_specs)
        args = [r_o, m_o, g_o, x, mod3_prev, w_out_bf] + common_args + list(tables) + tail_args
        out_specs = [tok(d)] + out_specs
        out_shape = [jax.ShapeDtypeStruct((b, s, d), F32)] + out_shape
    outs = pl.pallas_call(
        functools.partial(_layer_in_kernel, prev=not first, n_chunks=n_chunks),
        grid=(b, s // t),
        in_specs=in_specs,
        out_specs=out_specs,
        out_shape=out_shape,
        compiler_params=_cparams(("arbitrary", "arbitrary")),
        name="layer_in",
    )(*args)
    return [x] + list(outs) if first else list(outs)


def _out_proj_kernel(r_ref, m_ref, g_ref, x_ref, mod_ref, w_ref, fw_ref, o_ref):
    y = _dot(r_ref[0], w_ref[0:RET_WIDTH, :])
    y += _dot(m_ref[0], w_ref[RET_WIDTH:RET_WIDTH + MLA_WIDTH, :])
    y += _dot(g_ref[0], w_ref[RET_WIDTH + MLA_WIDTH:, :])
    gate = mod_ref[0, 2:3, :]
    x = x_ref[0] + gate * y
    o_ref[0] = x * lax.rsqrt(jnp.mean(x * x, axis=-1, keepdims=True) + EPS) * fw_ref[...]


def _out_proj(r_o, m_o, g_o, x, mod3, w_out_bf, final_row, t):
    b, s, d = x.shape

    def tok(w):
        return pl.BlockSpec((1, t, w), lambda i, j: (i, j, 0))

    return pl.pallas_call(
        _out_proj_kernel,
        grid=(b, s // t),
        in_specs=[tok(RET_WIDTH), tok(MLA_WIDTH), tok(GLA_WIDTH), tok(d),
                  pl.BlockSpec((1, 3, d), lambda i, j: (i, 0, 0)),
                  pl.BlockSpec(w_out_bf.shape, lambda i, j: (0, 0)),
                  pl.BlockSpec((1, d), lambda i, j: (0, 0))],
        out_specs=tok(d),
        out_shape=jax.ShapeDtypeStruct((b, s, d), F32),
        compiler_params=_cparams(("arbitrary", "arbitrary")),
        name="out_proj",
    )(r_o, m_o, g_o, x, mod3, w_out_bf, final_row)


def _pack_w_in(w_in_l):
    d = w_in_l.shape[0]
    cuts = np.cumsum([RET_WIDTH] * 4 + [MLA_Q_RANK, MLA_KV_RANK, MLA_ROPE, MLA_WIDTH]
                     + [GLA_KWIDTH, GLA_KWIDTH, GLA_WIDTH, GLA_GATE_RANK, GLA_WIDTH])[:-1]
    rq, rk, rv, rz, mq, kv, kr, mz, gq, gk, gv, gg, gz = jnp.split(w_in_l, [int(c) for c in cuts], axis=1)

    def halves_first(w):
        w4 = w.reshape(d, RET_HEADS, 2, RET_HEAD_DIM // 2)
        return jnp.swapaxes(w4, 1, 2).reshape(d, RET_WIDTH)

    hm = MLA_ROPE // 2
    kr_rot = jnp.concatenate([-kr[:, hm:], kr[:, :hm]], axis=1)
    mb_pad = jnp.zeros((d, LANE - MB_ROT_LANE - MLA_ROPE), F32)
    assert MB_ROT_LANE == MLA_ROPE + GLA_GATE_RANK
    packed = {
        OFF_MZ: mz, OFF_RQ: halves_first(rq), OFF_RK: halves_first(rk), OFF_RV: rv, OFF_RZ: rz,
        OFF_MQ: mq, OFF_GV: gv, OFF_GZ: gz, OFF_KV: kv,
        OFF_MB: jnp.concatenate([kr, gg, kr_rot, mb_pad], axis=1), OFF_GQ: gq, OFF_GK: gk,
    }
    pieces, pos = [], 0
    for off in sorted(packed):
        assert off == pos, "packed in-projection layout has a gap or overlap"
        pieces.append(packed[off])
        pos += packed[off].shape[1]
    assert pos == NP_COLS
    return jnp.concatenate(pieces, axis=1).astype(BF16)


def _pack_mla_q(w_uq_l, w_abs_l):
    r = w_uq_l.shape[0]
    hd = MLA_NOPE + MLA_ROPE
    hm = MLA_ROPE // 2
    w3 = w_uq_l.reshape(r, MLA_HEADS, hd)
    pe = w3[:, :, MLA_NOPE:]
    pad = jnp.zeros((r, MLA_HEADS, LANE - MLA_ROPE), F32)
    main = jnp.concatenate([jnp.moveaxis(w_abs_l, 0, 1), pe, pad], axis=-1)
    rot = jnp.concatenate([-pe[:, :, hm:], pe[:, :, :hm]], axis=-1)
    return (main.reshape(r, MLA_HEADS * 2 * LANE).astype(BF16),
            rot.reshape(r, MLA_HEADS * MLA_ROPE).astype(BF16))


def kernel(x, c, positions, norm_w, ada_w, ada_b, w_in, mla_q_norm, w_uq, mla_kv_norm, w_ukv,
           gla_w_g2, gla_b_g2, gla_norm, w_out, final_norm):
    b, s, d = x.shape
    depth = w_in.shape[0]
    t_tok = min(512, s)

    mod = _ada_mod(c, ada_w, ada_b).reshape(depth, b, 3, d)

    kv_hd = MLA_NOPE + MLA_V
    q_hd = MLA_NOPE + MLA_ROPE
    w_ukv4 = w_ukv.reshape(depth, MLA_KV_RANK, MLA_HEADS, kv_hd)
    wk_nope = jnp.moveaxis(w_ukv4[..., :MLA_NOPE], 2, 1)
    wuv = jnp.transpose(w_ukv4[..., MLA_NOPE:], (0, 2, 3, 1)).astype(BF16)
    wq_nope = jnp.moveaxis(
        w_uq.reshape(depth, MLA_Q_RANK, MLA_HEADS, q_hd)[..., :MLA_NOPE], 2, 1)
    w_abs = _absorb_weights(wq_nope, wk_nope)

    prev = None
    for l in range(depth):
        w_packed = _pack_w_in(w_in[l])
        wq_all, wrot = _pack_mla_q(w_uq[l], w_abs[l])
        w_g2p = jnp.zeros((LANE, GLA_KWIDTH), F32).at[MLA_ROPE:MLA_ROPE + GLA_GATE_RANK].set(
            gla_w_g2[l]).astype(BF16)

        front = (x, mod[l], norm_w[l].reshape(1, d), w_packed, mla_q_norm[l].reshape(1, MLA_Q_RANK),
                 mla_kv_norm[l].reshape(1, MLA_KV_RANK), wq_all, wrot, t_tok)
        if prev is None:
            x, proj, q_s, k_c, v_t, cos_r, sin_r, cos_m, sin_m = _layer_in(*front, positions=positions)
        else:
            x, proj, q_s, k_c, v_t = _layer_in(*front, tables=(cos_m, sin_m), prev=prev)
        r_o = _retention(proj, cos_r, sin_r)
        g_o = _gla(proj, w_g2p, gla_b_g2[l].reshape(1, GLA_KWIDTH),
                   jnp.tile(gla_norm[l], GLA_HEADS).reshape(1, GLA_WIDTH))
        m_o = _mla_attn(q_s, k_c, v_t, proj, wuv[l])
        prev = (r_o, m_o, g_o, mod[l], w_out[l].astype(BF16))
    r_o, m_o, g_o, mod_last, w_out_bf = prev
    return _out_proj(r_o, m_o, g_o, x, mod_last, w_out_bf, final_norm.reshape(1, d), t_tok)
```

```python
import functools

import numpy as np

import jax
import jax.numpy as jnp
from jax import lax
from jax.experimental import pallas as pl
from jax.experimental.pallas import tpu as pltpu

F32 = jnp.float32
BF16 = jnp.bfloat16

D_MODEL = 1024
CHUNK = 64
EPS = 1e-6
ROPE_THETA = 10000.0

RET_HEADS = 4
RET_HEAD_DIM = 64
RET_WIDTH = 256
MLA_HEADS = 8
MLA_NOPE = 64
MLA_ROPE = 32
MLA_V = 64
MLA_WIDTH = 512
MLA_Q_RANK = 256
MLA_KV_RANK = 128
GLA_HEADS = 4
GLA_DK = 32
GLA_DV = 64
GLA_KWIDTH = 128
GLA_WIDTH = 256
GLA_GATE_RANK = 16
GLA_TAU = 16.0
IN_COLS = 2736

LANE = 128

OFF_MZ = 0
OFF_RQ = 512
OFF_RK = 768
OFF_RV = 1024
OFF_RZ = 1280
OFF_MQ = 1536
OFF_GV = 1792
OFF_GZ = 2048
OFF_KV = 2304
OFF_MB = 2432
MB_ROT_LANE = 48
OFF_GQ = 2560
OFF_GK = 2688
NP_COLS = 2816

MIX_TILE = 256
MIX_SEQS = 4
ATT_T = 256
ATT_SEQS = 4
VT_ROWS = 144
LOG2E = 1.4426950408889634
MASK_NEG = -1e30
VMEM_LIMIT = 56 * 1024 * 1024


def _cparams(sem, flags=None):
    return pltpu.CompilerParams(dimension_semantics=sem, vmem_limit_bytes=VMEM_LIMIT, flags=flags)


def _dot(a, b):
    return jnp.dot(a, b, preferred_element_type=F32)


def _dot_nt(a, b):
    return lax.dot_general(a, b, (((1,), (1,)), ((), ())), preferred_element_type=F32)


def _dot_tn(a, b):
    return lax.dot_general(a, b, (((0,), (0,)), ((), ())), preferred_element_type=F32)


def _silu(x):
    return x / (1.0 + jnp.exp(-x))


def _ada_kernel(c_ref, w_ref, b_ref, o_ref):
    c = c_ref[...]
    o_ref[0] = _dot(_silu(c), w_ref[0]) + b_ref[0]


def _ada_mod(c, ada_w, ada_b):
    depth, d, d3 = ada_w.shape
    b = c.shape[0]
    nblk = d3 // d
    return pl.pallas_call(
        _ada_kernel,
        grid=(depth, nblk),
        in_specs=[
            pl.BlockSpec((b, d), lambda l, j: (0, 0)),
            pl.BlockSpec((1, d, d), lambda l, j: (l, 0, j)),
            pl.BlockSpec((1, 1, d), lambda l, j: (l, 0, j)),
        ],
        out_specs=pl.BlockSpec((1, b, d), lambda l, j: (l, 0, j)),
        out_shape=jax.ShapeDtypeStruct((depth, b, d3), F32),
        compiler_params=_cparams(("arbitrary", "arbitrary")),
        name="ada_mod",
    )(c, ada_w, ada_b.reshape(depth, 1, d3))


def _rope_rows(pos, inv_row):
    half_r = RET_HEAD_DIM // 2
    half_m = MLA_ROPE // 2
    ang = pos * inv_row
    lane = lax.broadcasted_iota(jnp.int32, ang.shape, 1)
    is_r = lane < half_r
    is_m = (lane >= half_r) & (lane < half_r + half_m)

    def tables(x):
        r = jnp.where(is_r, x, 0.0)
        r = r + pltpu.roll(r, half_r, 1)
        r = r + pltpu.roll(r, 2 * half_r, 1)
        m = pltpu.roll(jnp.where(is_m, x, 0.0), LANE - half_r, 1)
        m = m + pltpu.roll(m, half_m, 1)
        return r, m

    cos_r, cos_m = tables(jnp.cos(ang))
    sin_r, sin_m = tables(jnp.sin(ang))
    return cos_r, sin_r, cos_m, sin_m


def _rope_inv_row():
    half_r = RET_HEAD_DIM // 2
    half_m = MLA_ROPE // 2
    inv_r = ROPE_THETA ** (-jnp.arange(half_r, dtype=F32) / half_r)
    inv_m = ROPE_THETA ** (-jnp.arange(half_m, dtype=F32) / half_m)
    return jnp.concatenate(
        [inv_r, inv_m, jnp.zeros((LANE - half_r - half_m,), F32)]).reshape(1, LANE)


IN_NCHUNK = 512


def _retention_phases(q_ref, k_ref, v_ref, z_ref, cos_ref, sin_ref, dmat_ref, qw_ref, kw_ref,
                      sdec_ref, bd_ref, ones_ref, o_ref, state_ref):
    nb = q_ref.shape[0]

    def rope(ref, i):
        a = ref[i].astype(F32)
        cos, sin = cos_ref[i], sin_ref[i]
        x1 = a[:, :LANE]
        x2 = a[:, LANE:]
        return jnp.concatenate([x1 * cos - x2 * sin, x2 * cos + x1 * sin], axis=-1)

    lane = lax.broadcasted_iota(jnp.int32, (1, RET_WIDTH), 1)
    qk_head = (lane % LANE) // (RET_HEAD_DIM // 2)
    v_head = lane // RET_HEAD_DIM
    zero_bf = jnp.zeros((), BF16)

    def stack(a, head_of_lane):
        return jnp.concatenate(
            [jnp.where(head_of_lane == h, a, zero_bf) for h in range(RET_HEADS)], axis=0)

    seqs = range(nb)
    q = [rope(q_ref, i) for i in seqs]
    k = [rope(k_ref, i) for i in seqs]
    v = [v_ref[i] for i in seqs]
    state = [state_ref[i] for i in seqs]
    u = [_dot_tn((k[i] * kw_ref[...]).astype(BF16), v[i]) for i in seqs]
    inter = [_dot((q[i] * qw_ref[...]).astype(BF16), state[i].astype(BF16)) for i in seqs]
    yield
    scores = [_dot_nt(q[i].astype(BF16), stack(k[i].astype(BF16), qk_head)) for i in seqs]
    for i in seqs:
        state_ref[i] = state[i] * sdec_ref[...] + u[i] * bd_ref[...]
    yield
    intra = [_dot((scores[i] * dmat_ref[...]).astype(BF16), stack(v[i], v_head)) for i in seqs]
    yield
    o = [intra[i] + inter[i] for i in seqs]
    ms = [_dot((o[i] * o[i]).astype(BF16), ones_ref[...]) for i in seqs]
    yield
    for i in seqs:
        y = o[i] * lax.rsqrt(ms[i] + EPS)
        z = z_ref[i].astype(F32)
        o_ref[i] = (y * _silu(z)).astype(BF16)


def _retention_consts(t):
    f32 = F32
    h = RET_HEADS
    log_gamma = jnp.log1p(-jnp.exp2(-5.0 - jnp.arange(h, dtype=f32)))
    idx = jnp.arange(t, dtype=f32)
    dist = jnp.abs(idx[:, None] - idx[None, :])
    ci = jnp.arange(t) // CHUNK
    vis = (ci[None, :] <= ci[:, None]).astype(f32)
    k_scale = RET_HEAD_DIM ** -0.5
    dmat = jnp.exp(log_gamma[:, None, None] * dist[None]) * vis[None] * k_scale
    dmat = jnp.moveaxis(dmat, 0, 1).reshape(t, h * t)
    lane = jnp.arange(RET_WIDTH)
    qk_head = (lane % LANE) // (RET_HEAD_DIM // 2)
    v_head = lane // RET_HEAD_DIM
    lg_lane = log_gamma[qk_head]
    qw = jnp.exp((idx + 1.0)[:, None] * lg_lane[None, :])
    kw = jnp.exp((t - 1.0 - idx)[:, None] * lg_lane[None, :]) * k_scale
    bd = (qk_head[:, None] == v_head[None, :]).astype(f32)
    sdec = jnp.exp(t * lg_lane)[:, None] * bd
    ones = ((v_head[:, None] == v_head[None, :]).astype(f32) / RET_HEAD_DIM).astype(BF16)
    return dmat, qw, kw, sdec, bd, ones


def _gla_phases(q_ref, k_ref, v_ref, z_ref, mb_ref, wg_ref, bg_ref, ltri_ref, bdt_ref, ones_ref,
                nw_ref, o_ref, state_ref):
    nb, t = q_ref.shape[0], q_ref.shape[1]
    nchunk = t // CHUNK

    lane_k = lax.broadcasted_iota(jnp.int32, (1, GLA_KWIDTH), 1) // GLA_DK
    lane_v = lax.broadcasted_iota(jnp.int32, (1, GLA_WIDTH), 1) // GLA_DV
    row_i = lax.broadcasted_iota(jnp.int32, (CHUNK, GLA_HEADS * CHUNK), 0)
    col_j = lax.broadcasted_iota(jnp.int32, (CHUNK, GLA_HEADS * CHUNK), 1) % CHUNK
    causal = row_i >= col_j
    zero_bf = jnp.zeros((), BF16)
    ltri = ltri_ref[...]
    bdt = bdt_ref[...]
    ones = ones_ref[...]
    nw = nw_ref[...]

    def stack(a, head_of_lane):
        return jnp.concatenate(
            [jnp.where(head_of_lane == h, a, zero_bf) for h in range(GLA_HEADS)], axis=0)

    seqs = range(nb)
    g = [_dot(mb_ref[i], wg_ref[...]) + bg_ref[...] for i in seqs]
    yield
    log_a = [(jnp.minimum(x, 0.0) - jnp.log(1.0 + jnp.exp(-jnp.abs(x)))) / GLA_TAU for x in g]
    la_hi = [x.astype(BF16) for x in log_a]
    la_lo = [(x - h.astype(F32)).astype(BF16) for x, h in zip(log_a, la_hi)]
    cum = [_dot(ltri, h) + _dot(ltri, l) for h, l in zip(la_hi, la_lo)]
    yield
    q = [q_ref[i].astype(F32) for i in seqs]
    k = [k_ref[i].astype(F32) * (GLA_DK ** -0.5) for i in seqs]
    e_pos = [jnp.exp(x) for x in cum]
    e_neg = [jnp.exp(-x) for x in cum]
    q_pos = [(a * e).astype(BF16) for a, e in zip(q, e_pos)]
    q_neg = [(a * e).astype(BF16) for a, e in zip(q, e_neg)]
    k_neg = [(a * e).astype(BF16) for a, e in zip(k, e_neg)]
    k_pos = [(a * e).astype(BF16) for a, e in zip(k, e_pos)]

    units = [(i, slice(c * CHUNK, (c + 1) * CHUNK)) for i in seqs for c in range(nchunk)]
    v_c = [v_ref[i, r, :] for i, r in units]
    last = [cum[i][r.stop - 1:r.stop, :] for i, r in units]
    k_st = [(k[i][r] * jnp.exp(l - cum[i][r])).astype(BF16) for (i, r), l in zip(units, last)]
    u_t = [_dot_tn(vc, ks) for vc, ks in zip(v_c, k_st)]
    yield
    past = [_dot_nt(q_pos[i][r], stack(k_neg[i][r], lane_k)) for i, r in units]
    fut = [_dot_nt(q_neg[i][r], stack(k_pos[i][r], lane_k)) for i, r in units]
    yield
    attn = [jnp.where(causal, p, f).astype(BF16) for p, f in zip(past, fut)]
    intra = [_dot(a, stack(vc, lane_v)) for a, vc in zip(attn, v_c)]
    yield
    states = []
    for i in seqs:
        st = state_ref[i]
        for c in range(nchunk):
            states.append(st)
            st = st * jnp.exp(last[i * nchunk + c]) + u_t[i * nchunk + c] * bdt
        state_ref[i] = st
    inter = [_dot_nt(q_pos[i][r], st.astype(BF16)) for (i, r), st in zip(units, states)]
    yield
    o = [a + b for a, b in zip(intra, inter)]
    ms = [_dot((x * x).astype(BF16), ones) for x in o]
    yield
    for (i, r), x, m in zip(units, o, ms):
        y = x * lax.rsqrt(m + EPS) * nw
        z = z_ref[i, r, :].astype(F32)
        o_ref[i, r, :] = (y * _silu(z)).astype(BF16)


N_RET_IN = 12
N_RET_SEQ = 6
N_GLA_IN = 11
N_GLA_SEQ = 5
MIXER_ORDER = ("G0 R0 G0 R0 G0 G1 R1 G0 G1 R0 R1 G0 G1 G0 G1 R0 G0 G1 R1 R0 G0 G1 R1 G1 G1 R1").split()


def _linear_mixers_kernel(*refs):
    ret_in = refs[:N_RET_IN]
    gla_in = refs[N_RET_IN:N_RET_IN + N_GLA_IN]
    ret_out, gla_out, ret_state, gla_state = refs[N_RET_IN + N_GLA_IN:]
    nb = ret_out.shape[0]

    @pl.when(pl.program_id(1) == 0)
    def _():
        ret_state[...] = jnp.zeros_like(ret_state)
        gla_state[...] = jnp.zeros_like(gla_state)

    n_groups = 2 if nb % 2 == 0 else 1
    per = nb // n_groups
    phases = {}
    for grp in range(n_groups):
        def sub(r, grp=grp):
            return r.at[pl.ds(grp * per, per)]
        phases[f"R{grp}"] = _retention_phases(
            *[sub(r) for r in ret_in[:N_RET_SEQ]], *ret_in[N_RET_SEQ:], sub(ret_out), sub(ret_state))
        phases[f"G{grp}"] = _gla_phases(
            *[sub(r) for r in gla_in[:N_GLA_SEQ]], *gla_in[N_GLA_SEQ:], sub(gla_out), sub(gla_state))
    for who in MIXER_ORDER:
        if who in phases:
            next(phases[who], None)
    for gen in phases.values():
        assert next(gen, "done") == "done", "MIXER_ORDER does not cover every phase"


def _linear_mixers(proj, cos_r, sin_r, w_g2p, b_g2_row, gla_norm_row):
    b, s, _ = proj.shape
    t = min(MIX_TILE, s)
    nb = MIX_SEQS if b % MIX_SEQS == 0 else 1
    dmat, qw, kw, sdec, bd, r_ones = _retention_consts(t)
    idx = jnp.arange(t)
    ltri = ((idx[:, None] >= idx[None, :]) & (idx[:, None] // CHUNK == idx[None, :] // CHUNK)).astype(BF16)
    kh = jnp.arange(GLA_KWIDTH) // GLA_DK
    vh = jnp.arange(GLA_WIDTH) // GLA_DV
    bdt = (vh[:, None] == kh[None, :]).astype(F32)
    g_ones = ((vh[:, None] == vh[None, :]).astype(F32) / GLA_DV).astype(BF16)

    def col(off, w):
        return pl.BlockSpec((nb, t, w), lambda i, j, o=off // w: (i, j, o))

    def const(a):
        return pl.BlockSpec(a.shape, lambda i, j: (0, 0))

    tab = pl.BlockSpec((nb, t, LANE), lambda i, j: (i, j, 0))
    ret_consts = [dmat, qw, kw, sdec, bd, r_ones]
    gla_consts = [w_g2p, b_g2_row, ltri, bdt, g_ones, gla_norm_row]
    in_specs = ([col(OFF_RQ, RET_WIDTH), col(OFF_RK, RET_WIDTH), col(OFF_RV, RET_WIDTH),
                 col(OFF_RZ, RET_WIDTH), tab, tab] + [const(a) for a in ret_consts]
                + [col(OFF_GQ, GLA_KWIDTH), col(OFF_GK, GLA_KWIDTH), col(OFF_GV, GLA_WIDTH),
                   col(OFF_GZ, GLA_WIDTH), col(OFF_MB, LANE)] + [const(a) for a in gla_consts])
    args = [proj] * 4 + [cos_r, sin_r] + ret_consts + [proj] * 5 + gla_consts
    assert len(in_specs) == N_RET_IN + N_GLA_IN == len(args)
    return pl.pallas_call(
        _linear_mixers_kernel,
        grid=(b // nb, s // t),
        in_specs=in_specs,
        out_specs=[pl.BlockSpec((nb, t, RET_WIDTH), lambda i, j: (i, j, 0)),
                   pl.BlockSpec((nb, t, GLA_WIDTH), lambda i, j: (i, j, 0))],
        out_shape=[jax.ShapeDtypeStruct((b, s, RET_WIDTH), BF16),
                   jax.ShapeDtypeStruct((b, s, GLA_WIDTH), BF16)],
        scratch_shapes=[pltpu.VMEM((nb, RET_WIDTH, RET_WIDTH), F32),
                        pltpu.VMEM((nb, GLA_WIDTH, GLA_KWIDTH), F32)],
        compiler_params=_cparams(("arbitrary", "arbitrary")),
        name="linear_mixers",
    )(*args)


def _absorb_kernel(wq_ref, wk_ref, o_ref):
    o_ref[0, 0] = lax.dot_general(wq_ref[0, 0], wk_ref[0, 0], (((1,), (1,)), ((), ())),
                                  precision=lax.Precision.HIGHEST, preferred_element_type=F32)


def _absorb_weights(wq_nope, wk_nope):
    depth, h, r, dn = wq_nope.shape
    kr = wk_nope.shape[2]
    return pl.pallas_call(
        _absorb_kernel,
        grid=(depth, h),
        in_specs=[pl.BlockSpec((1, 1, r, dn), lambda l, i: (l, i, 0, 0)),
                  pl.BlockSpec((1, 1, kr, dn), lambda l, i: (l, i, 0, 0))],
        out_specs=pl.BlockSpec((1, 1, r, kr), lambda l, i: (l, i, 0, 0)),
        out_shape=jax.ShapeDtypeStruct((depth, h, r, kr), F32),
        compiler_params=_cparams(("arbitrary", "arbitrary")),
        name="mla_absorb",
    )(wq_nope, wk_nope)


def _mla_attn_kernel(q_ref, k_ref, vt_ref, z_ref, wuv_ref, o_ref, m_ref, acc_ref,
                     s0_ref, s1_ref, mt0_ref, mt1_ref):
    nb, tq = q_ref.shape[0], q_ref.shape[2]
    t = ATT_T
    n_tiles = pl.program_id(1) + 1
    units = [(i, h) for h in range(MLA_HEADS) for i in range(nb)]

    m_ref[...] = jnp.full_like(m_ref, -jnp.inf)
    acc_ref[...] = jnp.zeros_like(acc_ref)

    def key_start(kt):
        return pl.multiple_of(jnp.minimum(kt, n_tiles - 1) * t, t)

    def scores(i, h, k0, s_ref, mt_ref):
        cols = slice(h * tq, (h + 1) * tq)
        s = _dot_nt(k_ref[i, pl.ds(k0, t), :], q_ref[i, h])
        s_ref[i, :, cols] = s
        mt_ref[i, :, cols] = jnp.max(s, axis=0, keepdims=True)

    def accumulate(i, h, k0, s_ref, mt_ref):
        cols = slice(h * tq, (h + 1) * tq)
        m_prev = m_ref[i, :, cols]
        m_new = jnp.maximum(m_prev, mt_ref[i, :, cols])
        alpha = jnp.exp2(m_prev - m_new)
        p = jnp.exp2(s_ref[i, :, cols] - m_new).astype(BF16)
        pv = _dot(vt_ref[i, :, pl.ds(k0, t)], p)
        acc_ref[i, :, cols] = alpha * acc_ref[i, :, cols] + pv
        m_ref[i, :, cols] = m_new

    def overlapped(k_next, k_cur, s_next, mt_next, s_cur, mt_cur):
        for i, h in units:
            accumulate(i, h, k_cur, s_cur, mt_cur)
            scores(i, h, k_next, s_next, mt_next)

    for i, h in units:
        scores(i, h, 0, s0_ref, mt0_ref)

    def pair(j, carry):
        k_a, k_b, k_c = key_start(2 * j), key_start(2 * j + 1), key_start(2 * j + 2)
        overlapped(k_b, k_a, s1_ref, mt1_ref, s0_ref, mt0_ref)
        overlapped(k_c, k_b, s0_ref, mt0_ref, s1_ref, mt1_ref)
        return carry

    lax.fori_loop(0, n_tiles // 2, pair, 0)

    @pl.when(n_tiles % 2 == 1)
    def _():
        for i, h in units:
            accumulate(i, h, key_start(n_tiles - 1), s0_ref, mt0_ref)

    o_h = [(acc_ref[i, 0:LANE, h * tq:(h + 1) * tq]
            / acc_ref[i, LANE:LANE + 1, h * tq:(h + 1) * tq]).astype(BF16) for i, h in units]
    o_t = [_dot(wuv_ref[h], o) for (i, h), o in zip(units, o_h)]
    for i in range(nb):
        o_seq = jnp.concatenate([o for (ii, h), o in zip(units, o_t) if ii == i], axis=0)
        z = z_ref[i].astype(F32)
        o_ref[i] = (o_seq.T * _silu(z)).astype(BF16)


def _mla_attn(q_s, k_c, v_t, proj, wuv):
    b, h, s, _ = q_s.shape
    t = ATT_T
    nq = h * t
    vrows = v_t.shape[1]
    nb = ATT_SEQS if b % ATT_SEQS == 0 else 1
    return pl.pallas_call(
        _mla_attn_kernel,
        grid=(b // nb, s // t),
        in_specs=[pl.BlockSpec((nb, h, t, 2 * LANE), lambda i, j: (i, 0, j, 0)),
                  pl.BlockSpec((nb, s, 2 * LANE), lambda i, j: (i, 0, 0)),
                  pl.BlockSpec((nb, vrows, s), lambda i, j: (i, 0, 0)),
                  pl.BlockSpec((nb, t, MLA_WIDTH), lambda i, j: (i, j, OFF_MZ // MLA_WIDTH)),
                  pl.BlockSpec(wuv.shape, lambda i, j: (0, 0, 0))],
        out_specs=pl.BlockSpec((nb, t, MLA_WIDTH), lambda i, j: (i, j, 0)),
        out_shape=jax.ShapeDtypeStruct((b, s, MLA_WIDTH), BF16),
        scratch_shapes=[pltpu.VMEM((nb, 1, nq), F32), pltpu.VMEM((nb, vrows, nq), F32),
                        pltpu.VMEM((nb, t, nq), F32), pltpu.VMEM((nb, t, nq), F32),
                        pltpu.VMEM((nb, 1, nq), F32), pltpu.VMEM((nb, 1, nq), F32)],
        compiler_params=_cparams(("arbitrary", "arbitrary")),
        name="mla_attn",
    )(q_s, k_c, v_t, proj, wuv)


def _layer_in_kernel(*refs, prev, n_chunks):
    if prev:
        (r_ref, m_ref, g_ref, x_ref, modp_ref, wo_ref, mod_ref, nw_ref, w_ref, cos_ref, sin_ref,
         qnw_ref, kvnw_ref, wq_ref, wrot_ref, xo_ref, p_ref, q_out, k_out, vt_out) = refs
        y = _dot(r_ref[0], wo_ref[0:RET_WIDTH, :])
        y += _dot(m_ref[0], wo_ref[RET_WIDTH:RET_WIDTH + MLA_WIDTH, :])
        y += _dot(g_ref[0], wo_ref[RET_WIDTH + MLA_WIDTH:, :])
        x = x_ref[0] + modp_ref[0, 2:3, :] * y
        xo_ref[0] = x
        cos = cos_ref[0]
        sin = sin_ref[0]
    else:
        (x_ref, pos_ref, inv_ref, mod_ref, nw_ref, w_ref, qnw_ref, kvnw_ref, wq_ref, wrot_ref,
         p_ref, q_out, k_out, vt_out, cr_out, sr_out, cm_out, sm_out) = refs
        x = x_ref[0]
        cr_out[0], sr_out[0], cos, sin = _rope_rows(pos_ref[0].astype(F32), inv_ref[...])
        cm_out[0] = cos
        sm_out[0] = sin
    t = x.shape[0]
    shift = mod_ref[0, 0:1, :]
    scale = mod_ref[0, 1:2, :]
    hn = x * lax.rsqrt(jnp.mean(x * x, axis=-1, keepdims=True) + EPS) * nw_ref[...]
    act = (hn * (1.0 + scale) + shift).astype(BF16)

    def project(c0):
        c1 = min(c0 + IN_NCHUNK, NP_COLS)
        a = _dot(act, w_ref[:, c0:c1])
        p_ref[0, :, c0:c1] = a.astype(BF16)
        return a

    c_mq = OFF_MQ // IN_NCHUNK * IN_NCHUNK
    c_kv = OFF_KV // IN_NCHUNK * IN_NCHUNK
    assert c_kv == OFF_MB // IN_NCHUNK * IN_NCHUNK and c_mq != c_kv
    slab_q = project(c_mq)
    slab_kv = project(c_kv)
    rest = [c0 for c0 in range(0, NP_COLS, IN_NCHUNK) if c0 not in (c_mq, c_kv)]

    sm_scale = (MLA_NOPE + MLA_ROPE) ** -0.5 * LOG2E
    lat = slab_q[:, OFF_MQ - c_mq:OFF_MQ - c_mq + MLA_Q_RANK]
    lat = lat * lax.rsqrt(jnp.mean(lat * lat, axis=-1, keepdims=True) + EPS) * (qnw_ref[...] * sm_scale)
    lat = lat.astype(BF16)
    tok_chunk = (pl.program_id(1) * t + lax.broadcasted_iota(jnp.int32, (t, LANE), 0)) // CHUNK
    code_lane = lax.broadcasted_iota(jnp.int32, (t, LANE), 1) - MLA_ROPE
    q_code = jnp.where((code_lane > tok_chunk) & (code_lane < n_chunks), MASK_NEG, 0.0)
    k_code = jnp.where(code_lane == tok_chunk, 1.0, 0.0)

    kv = slab_kv[:, OFF_KV - c_kv:OFF_KV - c_kv + MLA_KV_RANK]
    kvn = kv * lax.rsqrt(jnp.mean(kv * kv, axis=-1, keepdims=True) + EPS) * kvnw_ref[...]
    k_out[0, :, :LANE] = kvn.astype(BF16)
    mb = slab_kv[:, OFF_MB - c_kv:OFF_MB - c_kv + LANE]
    mb_rot = pltpu.roll(mb, LANE - MB_ROT_LANE, 1)
    k_out[0, :, LANE:] = (mb * cos + mb_rot * sin + k_code).astype(BF16)
    vt_out[0, 0:LANE, :] = kvn.T.astype(BF16)
    pad_rows = vt_out.shape[1] - LANE
    row = lax.broadcasted_iota(jnp.int32, (pad_rows, t), 0)
    vt_out[0, LANE:, :] = jnp.where(row == 0, 1.0, 0.0).astype(BF16)

    rot_all = _dot(lat, wrot_ref[...])
    per_blk = LANE // MLA_ROPE
    heads_per_slab = -(-MLA_HEADS // max(len(rest), 1))
    for h in range(MLA_HEADS):
        if h % heads_per_slab == 0 and rest:
            project(rest.pop(0))
        main = _dot(lat, wq_ref[:, 2 * LANE * h:2 * LANE * (h + 1)])
        blk = rot_all[:, LANE * (h // per_blk):LANE * (h // per_blk + 1)]
        lane0 = MLA_ROPE * (h % per_blk)
        rot = blk if lane0 == 0 else pltpu.roll(blk, LANE - lane0, 1)
        q_out[0, h, :, :LANE] = main[:, :LANE].astype(BF16)
        q_out[0, h, :, LANE:] = (main[:, LANE:] * cos + rot * sin + q_code).astype(BF16)
    for c0 in rest:
        project(c0)


def _layer_in(x, mod3, norm_w_row, w_packed, qn_row, kvn_row, wq_all, wrot, t, *,
              positions=None, tables=None, prev=None):
    b, s, d = x.shape
    n_chunks = s // CHUNK
    assert MLA_ROPE + n_chunks <= LANE, "chunk-mask code does not fit the spare contraction lanes"
    first = prev is None
    assert first == (positions is not None) and first == (tables is None)

    def tok(w):
        return pl.BlockSpec((1, t, w), lambda i, j: (i, j, 0))

    def const(a):
        return pl.BlockSpec(a.shape, lambda i, j: (0,) * a.ndim)

    mod_spec = pl.BlockSpec((1, 3, d), lambda i, j: (i, 0, 0))
    common_specs = [mod_spec, const(norm_w_row), const(w_packed)]
    common_args = [mod3, norm_w_row, w_packed]
    tail_specs = [const(qn_row), const(kvn_row), const(wq_all), const(wrot)]
    tail_args = [qn_row, kvn_row, wq_all, wrot]
    out_specs = [tok(NP_COLS),
                 pl.BlockSpec((1, MLA_HEADS, t, 2 * LANE), lambda i, j: (i, 0, j, 0)),
                 tok(2 * LANE),
                 pl.BlockSpec((1, VT_ROWS, t), lambda i, j: (i, 0, j))]
    out_shape = [jax.ShapeDtypeStruct((b, s, NP_COLS), BF16),
                 jax.ShapeDtypeStruct((b, MLA_HEADS, s, 2 * LANE), BF16),
                 jax.ShapeDtypeStruct((b, s, 2 * LANE), BF16),
                 jax.ShapeDtypeStruct((b, VT_ROWS, s), BF16)]
    if first:
        inv_row = _rope_inv_row()
        in_specs = [tok(d), tok(1), const(inv_row)] + common_specs + tail_specs
        args = [x, positions.reshape(b, s, 1), inv_row] + common_args + tail_args
        out_specs = out_specs + [tok(LANE)] * 4
        out_shape = out_shape + [jax.ShapeDtypeStruct((b, s, LANE), F32)] * 4
    else:
        r_o, m_o, g_o, mod3_prev, w_out_bf = prev
        in_specs = ([tok(RET_WIDTH), tok(MLA_WIDTH), tok(GLA_WIDTH), tok(d), mod_spec, const(w_out_bf)]
                    + common_specs + [tok(LANE), tok(LANE)] + tail_specs)
        args = [r_o, m_o, g_o, x, mod3_prev, w_out_bf] + common_args + list(tables) + tail_args
        out_specs = [tok(d)] + out_specs
        out_shape = [jax.ShapeDtypeStruct((b, s, d), F32)] + out_shape
    outs = pl.pallas_call(
        functools.partial(_layer_in_kernel, prev=not first, n_chunks=n_chunks),
        grid=(b, s // t),
        in_specs=in_specs,
        out_specs=out_specs,
        out_shape=out_shape,
        compiler_params=_cparams(("arbitrary", "arbitrary")),
        name="layer_in",
    )(*args)
    return [x] + list(outs) if first else list(outs)


def _out_proj_kernel(r_ref, m_ref, g_ref, x_ref, mod_ref, w_ref, fw_ref, o_ref):
    y = _dot(r_ref[0], w_ref[0:RET_WIDTH, :])
    y += _dot(m_ref[0], w_ref[RET_WIDTH:RET_WIDTH + MLA_WIDTH, :])
    y += _dot(g_ref[0], w_ref[RET_WIDTH + MLA_WIDTH:, :])
    gate = mod_ref[0, 2:3, :]
    x = x_ref[0] + gate * y
    o_ref[0] = x * lax.rsqrt(jnp.mean(x * x, axis=-1, keepdims=True) + EPS) * fw_ref[...]


def _out_proj(r_o, m_o, g_o, x, mod3, w_out_bf, final_row, t):
    b, s, d = x.shape

    def tok(w):
        return pl.BlockSpec((1, t, w), lambda i, j: (i, j, 0))

    return pl.pallas_call(
        _out_proj_kernel,
        grid=(b, s // t),
        in_specs=[tok(RET_WIDTH), tok(MLA_WIDTH), tok(GLA_WIDTH), tok(d),
                  pl.BlockSpec((1, 3, d), lambda i, j: (i, 0, 0)),
                  pl.BlockSpec(w_out_bf.shape, lambda i, j: (0, 0)),
                  pl.BlockSpec((1, d), lambda i, j: (0, 0))],
        out_specs=tok(d),
        out_shape=jax.ShapeDtypeStruct((b, s, d), F32),
        compiler_params=_cparams(("arbitrary", "arbitrary")),
        name="out_proj",
    )(r_o, m_o, g_o, x, mod3, w_out_bf, final_row)


def _pack_w_in(w_in_l):
    d = w_in_l.shape[0]
    cuts = np.cumsum([RET_WIDTH] * 4 + [MLA_Q_RANK, MLA_KV_RANK, MLA_ROPE, MLA_WIDTH]
                     + [GLA_KWIDTH, GLA_KWIDTH, GLA_WIDTH, GLA_GATE_RANK, GLA_WIDTH])[:-1]
    rq, rk, rv, rz, mq, kv, kr, mz, gq, gk, gv, gg, gz = jnp.split(w_in_l, [int(c) for c in cuts], axis=1)

    def halves_first(w):
        w4 = w.reshape(d, RET_HEADS, 2, RET_HEAD_DIM // 2)
        return jnp.swapaxes(w4, 1, 2).reshape(d, RET_WIDTH)

    hm = MLA_ROPE // 2
    kr_rot = jnp.concatenate([-kr[:, hm:], kr[:, :hm]], axis=1)
    mb_pad = jnp.zeros((d, LANE - MB_ROT_LANE - MLA_ROPE), F32)
    assert MB_ROT_LANE == MLA_ROPE + GLA_GATE_RANK
    packed = {
        OFF_MZ: mz, OFF_RQ: halves_first(rq), OFF_RK: halves_first(rk), OFF_RV: rv, OFF_RZ: rz,
        OFF_MQ: mq, OFF_GV: gv, OFF_GZ: gz, OFF_KV: kv,
        OFF_MB: jnp.concatenate([kr, gg, kr_rot, mb_pad], axis=1), OFF_GQ: gq, OFF_GK: gk,
    }
    pieces, pos = [], 0
    for off in sorted(packed):
        assert off == pos, "packed in-projection layout has a gap or overlap"
        pieces.append(packed[off])
        pos += packed[off].shape[1]
    assert pos == NP_COLS
    return jnp.concatenate(pieces, axis=1).astype(BF16)


def _pack_mla_q(w_uq_l, w_abs_l):
    r = w_uq_l.shape[0]
    hd = MLA_NOPE + MLA_ROPE
    hm = MLA_ROPE // 2
    w3 = w_uq_l.reshape(r, MLA_HEADS, hd)
    pe = w3[:, :, MLA_NOPE:]
    pad = jnp.zeros((r, MLA_HEADS, LANE - MLA_ROPE), F32)
    main = jnp.concatenate([jnp.moveaxis(w_abs_l, 0, 1), pe, pad], axis=-1)
    rot = jnp.concatenate([-pe[:, :, hm:], pe[:, :, :hm]], axis=-1)
    return (main.reshape(r, MLA_HEADS * 2 * LANE).astype(BF16),
            rot.reshape(r, MLA_HEADS * MLA_ROPE).astype(BF16))


def kernel(x, c, positions, norm_w, ada_w, ada_b, w_in, mla_q_norm, w_uq, mla_kv_norm, w_ukv,
           gla_w_g2, gla_b_g2, gla_norm, w_out, final_norm):
    b, s, d = x.shape
    depth = w_in.shape[0]
    t_tok = min(512, s)

    mod = _ada_mod(c, ada_w, ada_b).reshape(depth, b, 3, d)

    kv_hd = MLA_NOPE + MLA_V
    q_hd = MLA_NOPE + MLA_ROPE
    w_ukv4 = w_ukv.reshape(depth, MLA_KV_RANK, MLA_HEADS, kv_hd)
    wk_nope = jnp.moveaxis(w_ukv4[..., :MLA_NOPE], 2, 1)
    wuv = jnp.transpose(w_ukv4[..., MLA_NOPE:], (0, 2, 3, 1)).astype(BF16)
    wq_nope = jnp.moveaxis(
        w_uq.reshape(depth, MLA_Q_RANK, MLA_HEADS, q_hd)[..., :MLA_NOPE], 2, 1)
    w_abs = _absorb_weights(wq_nope, wk_nope)

    prev = None
    for l in range(depth):
        w_packed = _pack_w_in(w_in[l])
        wq_all, wrot = _pack_mla_q(w_uq[l], w_abs[l])
        w_g2p = jnp.zeros((LANE, GLA_KWIDTH), F32).at[MLA_ROPE:MLA_ROPE + GLA_GATE_RANK].set(
            gla_w_g2[l]).astype(BF16)

        front = (x, mod[l], norm_w[l].reshape(1, d), w_packed, mla_q_norm[l].reshape(1, MLA_Q_RANK),
                 mla_kv_norm[l].reshape(1, MLA_KV_RANK), wq_all, wrot, t_tok)
        if prev is None:
            x, proj, q_s, k_c, v_t, cos_r, sin_r, cos_m, sin_m = _layer_in(*front, positions=positions)
        else:
            x, proj, q_s, k_c, v_t = _layer_in(*front, tables=(cos_m, sin_m), prev=prev)
        r_o, g_o = _linear_mixers(proj, cos_r, sin_r, w_g2p, gla_b_g2[l].reshape(1, GLA_KWIDTH),
                                  jnp.tile(gla_norm[l], GLA_HEADS).reshape(1, GLA_WIDTH))
        m_o = _mla_attn(q_s, k_c, v_t, proj, wuv[l])
        prev = (r_o, m_o, g_o, mod[l], w_out[l].astype(BF16))
    r_o, m_o, g_o, mod_last, w_out_bf = prev
    return _out_proj(r_o, m_o, g_o, x, mod_last, w_out_bf, final_norm.reshape(1, d), t_tok)
```

```python
import functools

import numpy as np

import jax
import jax.numpy as jnp
from jax import lax
from jax.experimental import pallas as pl
from jax.experimental.pallas import tpu as pltpu

F32 = jnp.float32
BF16 = jnp.bfloat16

D_MODEL = 1024
CHUNK = 64
EPS = 1e-6
ROPE_THETA = 10000.0

RET_HEADS = 4
RET_HEAD_DIM = 64
RET_WIDTH = 256
MLA_HEADS = 8
MLA_NOPE = 64
MLA_ROPE = 32
MLA_V = 64
MLA_WIDTH = 512
MLA_Q_RANK = 256
MLA_KV_RANK = 128
GLA_HEADS = 4
GLA_DK = 32
GLA_DV = 64
GLA_KWIDTH = 128
GLA_WIDTH = 256
GLA_GATE_RANK = 16
GLA_TAU = 16.0
IN_COLS = 2736

LANE = 128

OFF_MZ = 0
OFF_RQ = 512
OFF_RK = 768
OFF_RV = 1024
OFF_RZ = 1280
OFF_MQ = 1536
OFF_GV = 1792
OFF_GZ = 2048
OFF_KV = 2304
OFF_MB = 2432
MB_ROT_LANE = 48
OFF_GQ = 2560
OFF_GK = 2688
NP_COLS = 2816

MIX_TILE = 256
MIX_SEQS = 4
ATT_T = 256
ATT_SEQS = 4
VT_ROWS = 144
LOG2E = 1.4426950408889634
MASK_NEG = -1e30
OUT_TILE = 1024
VMEM_LIMIT = 56 * 1024 * 1024


def _cparams(sem, flags=None):
    return pltpu.CompilerParams(dimension_semantics=sem, vmem_limit_bytes=VMEM_LIMIT, flags=flags)


def _dot(a, b):
    return jnp.dot(a, b, preferred_element_type=F32)


def _dot_nt(a, b):
    return lax.dot_general(a, b, (((1,), (1,)), ((), ())), preferred_element_type=F32)


def _dot_tn(a, b):
    return lax.dot_general(a, b, (((0,), (0,)), ((), ())), preferred_element_type=F32)


def _silu(x):
    return x / (1.0 + jnp.exp(-x))


def _ada_kernel(c_ref, w_ref, b_ref, o_ref):
    c = c_ref[...]
    o_ref[0] = _dot(_silu(c), w_ref[0]) + b_ref[0]


def _ada_mod(c, ada_w, ada_b):
    depth, d, d3 = ada_w.shape
    b = c.shape[0]
    nblk = d3 // d
    return pl.pallas_call(
        _ada_kernel,
        grid=(depth, nblk),
        in_specs=[
            pl.BlockSpec((b, d), lambda l, j: (0, 0)),
            pl.BlockSpec((1, d, d), lambda l, j: (l, 0, j)),
            pl.BlockSpec((1, 1, d), lambda l, j: (l, 0, j)),
        ],
        out_specs=pl.BlockSpec((1, b, d), lambda l, j: (l, 0, j)),
        out_shape=jax.ShapeDtypeStruct((depth, b, d3), F32),
        compiler_params=_cparams(("arbitrary", "arbitrary")),
        name="ada_mod",
    )(c, ada_w, ada_b.reshape(depth, 1, d3))


def _rope_rows(pos, inv_row):
    half_r = RET_HEAD_DIM // 2
    half_m = MLA_ROPE // 2
    t2 = pos.shape[0] // 2
    lane = lax.broadcasted_iota(jnp.int32, (t2, LANE), 1)
    ang = jnp.where(lane < LANE // 2, pos[:t2], pos[t2:]) * inv_row
    is_r = lane < half_r
    is_m = (lane >= half_r) & (lane < half_r + half_m)

    def tables(x):
        r = jnp.where(is_r, x, 0.0)
        r = r + pltpu.roll(r, half_r, 1)
        r = r + pltpu.roll(r, 2 * half_r, 1)
        m = pltpu.roll(jnp.where(is_m, x, 0.0), LANE - half_r, 1)
        m = m + pltpu.roll(m, half_m, 1)
        return r, m

    def both_halves(x):
        r_lo, m_lo = tables(x)
        r_hi, m_hi = tables(pltpu.roll(x, LANE // 2, 1))
        return jnp.concatenate([r_lo, r_hi], axis=0), jnp.concatenate([m_lo, m_hi], axis=0)

    cos_r, cos_m = both_halves(jnp.cos(ang))
    sin_r, sin_m = both_halves(jnp.sin(ang))
    return cos_r, sin_r, cos_m, sin_m


def _rope_inv_row():
    half_r = RET_HEAD_DIM // 2
    half_m = MLA_ROPE // 2
    inv_r = ROPE_THETA ** (-jnp.arange(half_r, dtype=F32) / half_r)
    inv_m = ROPE_THETA ** (-jnp.arange(half_m, dtype=F32) / half_m)
    one = jnp.concatenate([inv_r, inv_m, jnp.zeros((LANE // 2 - half_r - half_m,), F32)])
    return jnp.tile(one, 2).reshape(1, LANE)


IN_NCHUNK = 512


def _retention_phases(q_ref, k_ref, v_ref, z_ref, cos_ref, sin_ref, dmat_ref, qw_ref, kw_ref,
                      sdec_ref, bd_ref, ones_ref, o_ref, state_ref):
    nb = q_ref.shape[0]

    def rope(ref, i):
        a = ref[i].astype(F32)
        cos, sin = cos_ref[i], sin_ref[i]
        x1 = a[:, :LANE]
        x2 = a[:, LANE:]
        return jnp.concatenate([x1 * cos - x2 * sin, x2 * cos + x1 * sin], axis=-1)

    lane = lax.broadcasted_iota(jnp.int32, (1, RET_WIDTH), 1)
    qk_head = (lane % LANE) // (RET_HEAD_DIM // 2)
    v_head = lane // RET_HEAD_DIM
    zero_bf = jnp.zeros((), BF16)

    def stack(a, head_of_lane):
        return jnp.concatenate(
            [jnp.where(head_of_lane == h, a, zero_bf) for h in range(RET_HEADS)], axis=0)

    seqs = range(nb)
    q = [rope(q_ref, i) for i in seqs]
    k = [rope(k_ref, i) for i in seqs]
    v = [v_ref[i] for i in seqs]
    state = [state_ref[i] for i in seqs]
    u = [_dot_tn((k[i] * kw_ref[...]).astype(BF16), v[i]) for i in seqs]
    inter = [_dot((q[i] * qw_ref[...]).astype(BF16), state[i].astype(BF16)) for i in seqs]
    yield
    scores = [_dot_nt(q[i].astype(BF16), stack(k[i].astype(BF16), qk_head)) for i in seqs]
    for i in seqs:
        state_ref[i] = state[i] * sdec_ref[...] + u[i] * bd_ref[...]
    yield
    intra = [_dot((scores[i] * dmat_ref[...]).astype(BF16), stack(v[i], v_head)) for i in seqs]
    yield
    o = [intra[i] + inter[i] for i in seqs]
    ms = [_dot((o[i] * o[i]).astype(BF16), ones_ref[...]) for i in seqs]
    yield
    for i in seqs:
        y = o[i] * lax.rsqrt(ms[i] + EPS)
        z = z_ref[i].astype(F32)
        o_ref[i] = (y * _silu(z)).astype(BF16)


def _retention_consts(t):
    f32 = F32
    h = RET_HEADS
    log_gamma = jnp.log1p(-jnp.exp2(-5.0 - jnp.arange(h, dtype=f32)))
    idx = jnp.arange(t, dtype=f32)
    dist = jnp.abs(idx[:, None] - idx[None, :])
    ci = jnp.arange(t) // CHUNK
    vis = (ci[None, :] <= ci[:, None]).astype(f32)
    k_scale = RET_HEAD_DIM ** -0.5
    dmat = jnp.exp(log_gamma[:, None, None] * dist[None]) * vis[None] * k_scale
    dmat = jnp.moveaxis(dmat, 0, 1).reshape(t, h * t)
    lane = jnp.arange(RET_WIDTH)
    qk_head = (lane % LANE) // (RET_HEAD_DIM // 2)
    v_head = lane // RET_HEAD_DIM
    lg_lane = log_gamma[qk_head]
    qw = jnp.exp((idx + 1.0)[:, None] * lg_lane[None, :])
    kw = jnp.exp((t - 1.0 - idx)[:, None] * lg_lane[None, :]) * k_scale
    bd = (qk_head[:, None] == v_head[None, :]).astype(f32)
    sdec = jnp.exp(t * lg_lane)[:, None] * bd
    ones = ((v_head[:, None] == v_head[None, :]).astype(f32) / RET_HEAD_DIM).astype(BF16)
    return dmat, qw, kw, sdec, bd, ones


def _gla_phases(q_ref, k_ref, v_ref, z_ref, mb_ref, wg_ref, bg_ref, ltri_ref, bdt_ref, ones_ref,
                nw_ref, o_ref, state_ref):
    nb, t = q_ref.shape[0], q_ref.shape[1]
    nchunk = t // CHUNK

    lane_k = lax.broadcasted_iota(jnp.int32, (1, GLA_KWIDTH), 1) // GLA_DK
    lane_v = lax.broadcasted_iota(jnp.int32, (1, GLA_WIDTH), 1) // GLA_DV
    row_i = lax.broadcasted_iota(jnp.int32, (CHUNK, GLA_HEADS * CHUNK), 0)
    col_j = lax.broadcasted_iota(jnp.int32, (CHUNK, GLA_HEADS * CHUNK), 1) % CHUNK
    causal = row_i >= col_j
    zero_bf = jnp.zeros((), BF16)
    ltri = ltri_ref[...]
    bdt = bdt_ref[...]
    ones = ones_ref[...]
    nw = nw_ref[...]

    def stack(a, head_of_lane):
        return jnp.concatenate(
            [jnp.where(head_of_lane == h, a, zero_bf) for h in range(GLA_HEADS)], axis=0)

    seqs = range(nb)
    g = [_dot(mb_ref[i], wg_ref[...]) + bg_ref[...] for i in seqs]
    yield
    log_a = [(jnp.minimum(x, 0.0) - jnp.log(1.0 + jnp.exp(-jnp.abs(x)))) / GLA_TAU for x in g]
    la_hi = [x.astype(BF16) for x in log_a]
    la_lo = [(x - h.astype(F32)).astype(BF16) for x, h in zip(log_a, la_hi)]
    cum = [_dot(ltri, h) + _dot(ltri, l) for h, l in zip(la_hi, la_lo)]
    yield
    q = [q_ref[i].astype(F32) for i in seqs]
    k = [k_ref[i].astype(F32) * (GLA_DK ** -0.5) for i in seqs]
    e_pos = [jnp.exp(x) for x in cum]
    e_neg = [jnp.exp(-x) for x in cum]
    q_pos = [(a * e).astype(BF16) for a, e in zip(q, e_pos)]
    q_neg = [(a * e).astype(BF16) for a, e in zip(q, e_neg)]
    k_neg = [(a * e).astype(BF16) for a, e in zip(k, e_neg)]
    k_pos = [(a * e).astype(BF16) for a, e in zip(k, e_pos)]

    units = [(i, slice(c * CHUNK, (c + 1) * CHUNK)) for i in seqs for c in range(nchunk)]
    v_c = [v_ref[i, r, :] for i, r in units]
    last = [cum[i][r.stop - 1:r.stop, :] for i, r in units]
    k_st = [(k[i][r] * jnp.exp(l - cum[i][r])).astype(BF16) for (i, r), l in zip(units, last)]
    u_t = [_dot_tn(vc, ks) for vc, ks in zip(v_c, k_st)]
    yield
    past = [_dot_nt(q_pos[i][r], stack(k_neg[i][r], lane_k)) for i, r in units]
    fut = [_dot_nt(q_neg[i][r], stack(k_pos[i][r], lane_k)) for i, r in units]
    yield
    attn = [jnp.where(causal, p, f).astype(BF16) for p, f in zip(past, fut)]
    intra = [_dot(a, stack(vc, lane_v)) for a, vc in zip(attn, v_c)]
    yield
    states = []
    for i in seqs:
        st = state_ref[i]
        for c in range(nchunk):
            states.append(st)
            st = st * jnp.exp(last[i * nchunk + c]) + u_t[i * nchunk + c] * bdt
        state_ref[i] = st
    inter = [_dot_nt(q_pos[i][r], st.astype(BF16)) for (i, r), st in zip(units, states)]
    yield
    o = [a + b for a, b in zip(intra, inter)]
    ms = [_dot((x * x).astype(BF16), ones) for x in o]
    yield
    for (i, r), x, m in zip(units, o, ms):
        y = x * lax.rsqrt(m + EPS) * nw
        z = z_ref[i, r, :].astype(F32)
        o_ref[i, r, :] = (y * _silu(z)).astype(BF16)


N_RET_IN = 12
N_RET_SEQ = 6
N_GLA_IN = 11
N_GLA_SEQ = 5
MIXER_ORDER = ("G0 R0 G0 R0 G0 G1 R1 G0 G1 R0 R1 G0 G1 G0 G1 R0 G0 G1 R1 R0 G0 G1 R1 G1 G1 R1").split()


def _linear_mixers_kernel(*refs):
    ret_in = refs[:N_RET_IN]
    gla_in = refs[N_RET_IN:N_RET_IN + N_GLA_IN]
    ret_out, gla_out, ret_state, gla_state = refs[N_RET_IN + N_GLA_IN:]
    nb = ret_out.shape[0]

    @pl.when(pl.program_id(1) == 0)
    def _():
        ret_state[...] = jnp.zeros_like(ret_state)
        gla_state[...] = jnp.zeros_like(gla_state)

    n_groups = 2 if nb % 2 == 0 else 1
    per = nb // n_groups
    phases = {}
    for grp in range(n_groups):
        def sub(r, grp=grp):
            return r.at[pl.ds(grp * per, per)]
        phases[f"R{grp}"] = _retention_phases(
            *[sub(r) for r in ret_in[:N_RET_SEQ]], *ret_in[N_RET_SEQ:], sub(ret_out), sub(ret_state))
        phases[f"G{grp}"] = _gla_phases(
            *[sub(r) for r in gla_in[:N_GLA_SEQ]], *gla_in[N_GLA_SEQ:], sub(gla_out), sub(gla_state))
    for who in MIXER_ORDER:
        if who in phases:
            next(phases[who], None)
    for gen in phases.values():
        assert next(gen, "done") == "done", "MIXER_ORDER does not cover every phase"


def _linear_mixers(proj, cos_r, sin_r, w_g2p, b_g2_row, gla_norm_row):
    b, s, _ = proj.shape
    t = min(MIX_TILE, s)
    nb = MIX_SEQS if b % MIX_SEQS == 0 else 1
    dmat, qw, kw, sdec, bd, r_ones = _retention_consts(t)
    idx = jnp.arange(t)
    ltri = ((idx[:, None] >= idx[None, :]) & (idx[:, None] // CHUNK == idx[None, :] // CHUNK)).astype(BF16)
    kh = jnp.arange(GLA_KWIDTH) // GLA_DK
    vh = jnp.arange(GLA_WIDTH) // GLA_DV
    bdt = (vh[:, None] == kh[None, :]).astype(F32)
    g_ones = ((vh[:, None] == vh[None, :]).astype(F32) / GLA_DV).astype(BF16)

    def col(off, w):
        return pl.BlockSpec((nb, t, w), lambda i, j, o=off // w: (i, j, o))

    def const(a):
        return pl.BlockSpec(a.shape, lambda i, j: (0, 0))

    tab = pl.BlockSpec((nb, t, LANE), lambda i, j: (i, j, 0))
    ret_consts = [dmat, qw, kw, sdec, bd, r_ones]
    gla_consts = [w_g2p, b_g2_row, ltri, bdt, g_ones, gla_norm_row]
    in_specs = ([col(OFF_RQ, RET_WIDTH), col(OFF_RK, RET_WIDTH), col(OFF_RV, RET_WIDTH),
                 col(OFF_RZ, RET_WIDTH), tab, tab] + [const(a) for a in ret_consts]
                + [col(OFF_GQ, GLA_KWIDTH), col(OFF_GK, GLA_KWIDTH), col(OFF_GV, GLA_WIDTH),
                   col(OFF_GZ, GLA_WIDTH), col(OFF_MB, LANE)] + [const(a) for a in gla_consts])
    args = [proj] * 4 + [cos_r, sin_r] + ret_consts + [proj] * 5 + gla_consts
    assert len(in_specs) == N_RET_IN + N_GLA_IN == len(args)
    return pl.pallas_call(
        _linear_mixers_kernel,
        grid=(b // nb, s // t),
        in_specs=in_specs,
        out_specs=[pl.BlockSpec((nb, t, RET_WIDTH), lambda i, j: (i, j, 0)),
                   pl.BlockSpec((nb, t, GLA_WIDTH), lambda i, j: (i, j, 0))],
        out_shape=[jax.ShapeDtypeStruct((b, s, RET_WIDTH), BF16),
                   jax.ShapeDtypeStruct((b, s, GLA_WIDTH), BF16)],
        scratch_shapes=[pltpu.VMEM((nb, RET_WIDTH, RET_WIDTH), F32),
                        pltpu.VMEM((nb, GLA_WIDTH, GLA_KWIDTH), F32)],
        compiler_params=_cparams(("arbitrary", "arbitrary")),
        name="linear_mixers",
    )(*args)


def _absorb_kernel(wq_ref, wk_ref, o_ref):
    o_ref[0, 0] = lax.dot_general(wq_ref[0, 0], wk_ref[0, 0], (((1,), (1,)), ((), ())),
                                  precision=lax.Precision.HIGHEST, preferred_element_type=F32)


def _absorb_weights(wq_nope, wk_nope):
    depth, h, r, dn = wq_nope.shape
    kr = wk_nope.shape[2]
    return pl.pallas_call(
        _absorb_kernel,
        grid=(depth, h),
        in_specs=[pl.BlockSpec((1, 1, r, dn), lambda l, i: (l, i, 0, 0)),
                  pl.BlockSpec((1, 1, kr, dn), lambda l, i: (l, i, 0, 0))],
        out_specs=pl.BlockSpec((1, 1, r, kr), lambda l, i: (l, i, 0, 0)),
        out_shape=jax.ShapeDtypeStruct((depth, h, r, kr), F32),
        compiler_params=_cparams(("arbitrary", "arbitrary")),
        name="mla_absorb",
    )(wq_nope, wk_nope)


def _mla_attn_kernel(q_ref, k_ref, vt_ref, z_ref, wuv_ref, o_ref, m_ref, acc_ref,
                     s0_ref, s1_ref, mt0_ref, mt1_ref):
    nb, tq = q_ref.shape[0], q_ref.shape[2]
    t = ATT_T
    n_tiles = pl.program_id(1) + 1
    units = [(i, h) for h in range(MLA_HEADS) for i in range(nb)]

    def key_start(kt):
        return pl.multiple_of(jnp.minimum(kt, n_tiles - 1) * t, t)

    def scores(i, h, k0, s_ref, mt_ref):
        cols = slice(h * tq, (h + 1) * tq)
        s = _dot_nt(k_ref[i, pl.ds(k0, t), :], q_ref[i, h])
        s_ref[i, :, cols] = s
        mt_ref[i, :, cols] = jnp.max(s, axis=0, keepdims=True)

    def accumulate(i, h, k0, s_ref, mt_ref, first=False):
        cols = slice(h * tq, (h + 1) * tq)
        m_new = mt_ref[i, :, cols]
        if not first:
            m_prev = m_ref[i, :, cols]
            m_new = jnp.maximum(m_prev, m_new)
        p = jnp.exp2(s_ref[i, :, cols] - m_new).astype(BF16)
        pv = _dot(vt_ref[i, :, pl.ds(k0, t)], p)
        if not first:
            pv = jnp.exp2(m_prev - m_new) * acc_ref[i, :, cols] + pv
        acc_ref[i, :, cols] = pv
        m_ref[i, :, cols] = m_new

    def overlapped(k_next, k_cur, s_next, mt_next, s_cur, mt_cur, first=False):
        for i, h in units:
            accumulate(i, h, k_cur, s_cur, mt_cur, first)
            scores(i, h, k_next, s_next, mt_next)

    for i, h in units:
        scores(i, h, 0, s0_ref, mt0_ref)
    overlapped(key_start(1), 0, s1_ref, mt1_ref, s0_ref, mt0_ref, first=True)

    def pair(j, carry):
        k_a, k_b, k_c = key_start(2 * j + 1), key_start(2 * j + 2), key_start(2 * j + 3)
        overlapped(k_b, k_a, s0_ref, mt0_ref, s1_ref, mt1_ref)
        overlapped(k_c, k_b, s1_ref, mt1_ref, s0_ref, mt0_ref)
        return carry

    lax.fori_loop(0, (n_tiles - 1) // 2, pair, 0)

    @pl.when((n_tiles - 1) % 2 == 1)
    def _():
        for i, h in units:
            accumulate(i, h, key_start(n_tiles - 1), s1_ref, mt1_ref)

    o_h = [(acc_ref[i, 0:LANE, h * tq:(h + 1) * tq]
            * (1.0 / acc_ref[i, LANE:LANE + 1, h * tq:(h + 1) * tq])).astype(BF16) for i, h in units]
    o_t = [_dot(wuv_ref[h], o) for (i, h), o in zip(units, o_h)]
    for i in range(nb):
        o_seq = jnp.concatenate([o for (ii, h), o in zip(units, o_t) if ii == i], axis=0)
        z = z_ref[i].astype(F32)
        o_ref[i] = (o_seq.T * _silu(z)).astype(BF16)


def _mla_attn(q_s, k_c, v_t, proj, wuv):
    b, h, s, _ = q_s.shape
    t = ATT_T
    nq = h * t
    vrows = v_t.shape[1]
    nb = ATT_SEQS if b % ATT_SEQS == 0 else 1
    return pl.pallas_call(
        _mla_attn_kernel,
        grid=(b // nb, s // t),
        in_specs=[pl.BlockSpec((nb, h, t, 2 * LANE), lambda i, j: (i, 0, j, 0)),
                  pl.BlockSpec((nb, s, 2 * LANE), lambda i, j: (i, 0, 0)),
                  pl.BlockSpec((nb, vrows, s), lambda i, j: (i, 0, 0)),
                  pl.BlockSpec((nb, t, MLA_WIDTH), lambda i, j: (i, j, OFF_MZ // MLA_WIDTH)),
                  pl.BlockSpec(wuv.shape, lambda i, j: (0, 0, 0))],
        out_specs=pl.BlockSpec((nb, t, MLA_WIDTH), lambda i, j: (i, j, 0)),
        out_shape=jax.ShapeDtypeStruct((b, s, MLA_WIDTH), BF16),
        scratch_shapes=[pltpu.VMEM((nb, 1, nq), F32), pltpu.VMEM((nb, vrows, nq), F32),
                        pltpu.VMEM((nb, t, nq), F32), pltpu.VMEM((nb, t, nq), F32),
                        pltpu.VMEM((nb, 1, nq), F32), pltpu.VMEM((nb, 1, nq), F32)],
        compiler_params=_cparams(("arbitrary", "arbitrary")),
        name="mla_attn",
    )(q_s, k_c, v_t, proj, wuv)


def _layer_in_kernel(*refs, prev, n_chunks):
    if prev:
        (r_ref, m_ref, g_ref, x_ref, modp_ref, wo_ref, mod_ref, nw_ref, w_ref, cos_ref, sin_ref,
         qnw_ref, kvnw_ref, wq_ref, wrot_ref, xo_ref, p_ref, q_out, k_out, vt_out) = refs
        y = _dot(r_ref[0], wo_ref[0:RET_WIDTH, :])
        y += _dot(m_ref[0], wo_ref[RET_WIDTH:RET_WIDTH + MLA_WIDTH, :])
        y += _dot(g_ref[0], wo_ref[RET_WIDTH + MLA_WIDTH:, :])
        x = x_ref[0] + modp_ref[0, 2:3, :] * y
        xo_ref[0] = x
        cos = cos_ref[0]
        sin = sin_ref[0]
    else:
        (x_ref, pos_ref, inv_ref, mod_ref, nw_ref, w_ref, qnw_ref, kvnw_ref, wq_ref, wrot_ref,
         p_ref, q_out, k_out, vt_out, cr_out, sr_out, cm_out, sm_out) = refs
        x = x_ref[0]
        cr_out[0], sr_out[0], cos, sin = _rope_rows(pos_ref[0].astype(F32), inv_ref[...])
        cm_out[0] = cos
        sm_out[0] = sin
    t = x.shape[0]
    shift = mod_ref[0, 0:1, :]
    scale = mod_ref[0, 1:2, :]
    hn = x * lax.rsqrt(jnp.mean(x * x, axis=-1, keepdims=True) + EPS) * nw_ref[...]
    act = (hn * (1.0 + scale) + shift).astype(BF16)

    def project(c0):
        c1 = min(c0 + IN_NCHUNK, NP_COLS)
        a = _dot(act, w_ref[:, c0:c1])
        p_ref[0, :, c0:c1] = a.astype(BF16)
        return a

    c_mq = OFF_MQ // IN_NCHUNK * IN_NCHUNK
    c_kv = OFF_KV // IN_NCHUNK * IN_NCHUNK
    assert c_kv == OFF_MB // IN_NCHUNK * IN_NCHUNK and c_mq != c_kv
    slab_q = project(c_mq)
    slab_kv = project(c_kv)
    rest = [c0 for c0 in range(0, NP_COLS, IN_NCHUNK) if c0 not in (c_mq, c_kv)]

    sm_scale = (MLA_NOPE + MLA_ROPE) ** -0.5 * LOG2E
    lat = slab_q[:, OFF_MQ - c_mq:OFF_MQ - c_mq + MLA_Q_RANK]
    lat = lat * lax.rsqrt(jnp.mean(lat * lat, axis=-1, keepdims=True) + EPS) * (qnw_ref[...] * sm_scale)
    lat = lat.astype(BF16)
    tok_chunk = (pl.program_id(1) * t + lax.broadcasted_iota(jnp.int32, (t, LANE), 0)) // CHUNK
    code_lane = lax.broadcasted_iota(jnp.int32, (t, LANE), 1) - MLA_ROPE
    q_code = jnp.where((code_lane > tok_chunk) & (code_lane < n_chunks), MASK_NEG, 0.0)
    k_code = jnp.where(code_lane == tok_chunk, 1.0, 0.0)

    kv = slab_kv[:, OFF_KV - c_kv:OFF_KV - c_kv + MLA_KV_RANK]
    kvn = kv * lax.rsqrt(jnp.mean(kv * kv, axis=-1, keepdims=True) + EPS) * kvnw_ref[...]
    k_out[0, :, :LANE] = kvn.astype(BF16)
    mb = slab_kv[:, OFF_MB - c_kv:OFF_MB - c_kv + LANE]
    mb_rot = pltpu.roll(mb, LANE - MB_ROT_LANE, 1)
    k_out[0, :, LANE:] = (mb * cos + mb_rot * sin + k_code).astype(BF16)
    vt_out[0, 0:LANE, :] = kvn.T.astype(BF16)
    pad_rows = vt_out.shape[1] - LANE
    row = lax.broadcasted_iota(jnp.int32, (pad_rows, t), 0)
    vt_out[0, LANE:, :] = jnp.where(row == 0, 1.0, 0.0).astype(BF16)

    rot_all = _dot(lat, wrot_ref[...])
    per_blk = LANE // MLA_ROPE
    heads_per_slab = -(-MLA_HEADS // max(len(rest), 1))
    for h in range(MLA_HEADS):
        if h % heads_per_slab == 0 and rest:
            project(rest.pop(0))
        main = _dot(lat, wq_ref[:, 2 * LANE * h:2 * LANE * (h + 1)])
        blk = rot_all[:, LANE * (h // per_blk):LANE * (h // per_blk + 1)]
        lane0 = MLA_ROPE * (h % per_blk)
        rot = blk if lane0 == 0 else pltpu.roll(blk, LANE - lane0, 1)
        q_out[0, h, :, :LANE] = main[:, :LANE].astype(BF16)
        q_out[0, h, :, LANE:] = (main[:, LANE:] * cos + rot * sin + q_code).astype(BF16)
    for c0 in rest:
        project(c0)


def _layer_in(x, mod3, norm_w_row, w_packed, qn_row, kvn_row, wq_all, wrot, t, *,
              positions=None, tables=None, prev=None):
    b, s, d = x.shape
    n_chunks = s // CHUNK
    assert MLA_ROPE + n_chunks <= LANE, "chunk-mask code does not fit the spare contraction lanes"
    first = prev is None
    assert first == (positions is not None) and first == (tables is None)

    def tok(w):
        return pl.BlockSpec((1, t, w), lambda i, j: (i, j, 0))

    def const(a):
        return pl.BlockSpec(a.shape, lambda i, j: (0,) * a.ndim)

    mod_spec = pl.BlockSpec((1, 3, d), lambda i, j: (i, 0, 0))
    common_specs = [mod_spec, const(norm_w_row), const(w_packed)]
    common_args = [mod3, norm_w_row, w_packed]
    tail_specs = [const(qn_row), const(kvn_row), const(wq_all), const(wrot)]
    tail_args = [qn_row, kvn_row, wq_all, wrot]
    out_specs = [tok(NP_COLS),
                 pl.BlockSpec((1, MLA_HEADS, t, 2 * LANE), lambda i, j: (i, 0, j, 0)),
                 tok(2 * LANE),
                 pl.BlockSpec((1, VT_ROWS, t), lambda i, j: (i, 0, j))]
    out_shape = [jax.ShapeDtypeStruct((b, s, NP_COLS), BF16),
                 jax.ShapeDtypeStruct((b, MLA_HEADS, s, 2 * LANE), BF16),
                 jax.ShapeDtypeStruct((b, s, 2 * LANE), BF16),
                 jax.ShapeDtypeStruct((b, VT_ROWS, s), BF16)]
    if first:
        inv_row = _rope_inv_row()
        in_specs = [tok(d), tok(1), const(inv_row)] + common_specs + tail_specs
        args = [x, positions.reshape(b, s, 1), inv_row] + common_args + tail_args
        out_specs = out_specs + [tok(LANE)] * 4
        out_shape = out_shape + [jax.ShapeDtypeStruct((b, s, LANE), F32)] * 4
    else:
        r_o, m_o, g_o, mod3_prev, w_out_bf = prev
        in_specs = ([tok(RET_WIDTH), tok(MLA_WIDTH), tok(GLA_WIDTH), tok(d), mod_spec, const(w_out_bf)]
                    + common_specs + [tok(LANE), tok(LANE)] + tail_specs)
        args = [r_o, m_o, g_o, x, mod3_prev, w_out_bf] + common_args + list(tables) + tail_args
        out_specs = [tok(d)] + out_specs
        out_shape = [jax.ShapeDtypeStruct((b, s, d), F32)] + out_shape
    outs = pl.pallas_call(
        functools.partial(_layer_in_kernel, prev=not first, n_chunks=n_chunks),
        grid=(b, s // t),
        in_specs=in_specs,
        out_specs=out_specs,
        out_shape=out_shape,
        compiler_params=_cparams(("arbitrary", "arbitrary")),
        name="layer_in",
    )(*args)
    return [x] + list(outs) if first else list(outs)


def _out_proj_kernel(r_ref, m_ref, g_ref, x_ref, mod_ref, w_ref, fw_ref, o_ref):
    y = _dot(r_ref[0], w_ref[0:RET_WIDTH, :])
    y += _dot(m_ref[0], w_ref[RET_WIDTH:RET_WIDTH + MLA_WIDTH, :])
    y += _dot(g_ref[0], w_ref[RET_WIDTH + MLA_WIDTH:, :])
    gate = mod_ref[0, 2:3, :]
    x = x_ref[0] + gate * y
    o_ref[0] = x * lax.rsqrt(jnp.mean(x * x, axis=-1, keepdims=True) + EPS) * fw_ref[...]


def _out_proj(r_o, m_o, g_o, x, mod3, w_out_bf, final_row, t):
    b, s, d = x.shape

    def tok(w):
        return pl.BlockSpec((1, t, w), lambda i, j: (i, j, 0))

    return pl.pallas_call(
        _out_proj_kernel,
        grid=(b, s // t),
        in_specs=[tok(RET_WIDTH), tok(MLA_WIDTH), tok(GLA_WIDTH), tok(d),
                  pl.BlockSpec((1, 3, d), lambda i, j: (i, 0, 0)),
                  pl.BlockSpec(w_out_bf.shape, lambda i, j: (0, 0)),
                  pl.BlockSpec((1, d), lambda i, j: (0, 0))],
        out_specs=tok(d),
        out_shape=jax.ShapeDtypeStruct((b, s, d), F32),
        compiler_params=_cparams(("arbitrary", "arbitrary")),
        name="out_proj",
    )(r_o, m_o, g_o, x, mod3, w_out_bf, final_row)


def _pack_w_in(w_in_l):
    d = w_in_l.shape[0]
    cuts = np.cumsum([RET_WIDTH] * 4 + [MLA_Q_RANK, MLA_KV_RANK, MLA_ROPE, MLA_WIDTH]
                     + [GLA_KWIDTH, GLA_KWIDTH, GLA_WIDTH, GLA_GATE_RANK, GLA_WIDTH])[:-1]
    rq, rk, rv, rz, mq, kv, kr, mz, gq, gk, gv, gg, gz = jnp.split(w_in_l, [int(c) for c in cuts], axis=1)

    def halves_first(w):
        w4 = w.reshape(d, RET_HEADS, 2, RET_HEAD_DIM // 2)
        return jnp.swapaxes(w4, 1, 2).reshape(d, RET_WIDTH)

    hm = MLA_ROPE // 2
    kr_rot = jnp.concatenate([-kr[:, hm:], kr[:, :hm]], axis=1)
    mb_pad = jnp.zeros((d, LANE - MB_ROT_LANE - MLA_ROPE), F32)
    assert MB_ROT_LANE == MLA_ROPE + GLA_GATE_RANK
    packed = {
        OFF_MZ: mz, OFF_RQ: halves_first(rq), OFF_RK: halves_first(rk), OFF_RV: rv, OFF_RZ: rz,
        OFF_MQ: mq, OFF_GV: gv, OFF_GZ: gz, OFF_KV: kv,
        OFF_MB: jnp.concatenate([kr, gg, kr_rot, mb_pad], axis=1), OFF_GQ: gq, OFF_GK: gk,
    }
    pieces, pos = [], 0
    for off in sorted(packed):
        assert off == pos, "packed in-projection layout has a gap or overlap"
        pieces.append(packed[off])
        pos += packed[off].shape[1]
    assert pos == NP_COLS
    return jnp.concatenate(pieces, axis=1).astype(BF16)


def _pack_mla_q(w_uq_l, w_abs_l):
    r = w_uq_l.shape[0]
    hd = MLA_NOPE + MLA_ROPE
    hm = MLA_ROPE // 2
    w3 = w_uq_l.reshape(r, MLA_HEADS, hd)
    pe = w3[:, :, MLA_NOPE:]
    pad = jnp.zeros((r, MLA_HEADS, LANE - MLA_ROPE), F32)
    main = jnp.concatenate([jnp.moveaxis(w_abs_l, 0, 1), pe, pad], axis=-1)
    rot = jnp.concatenate([-pe[:, :, hm:], pe[:, :, :hm]], axis=-1)
    return (main.reshape(r, MLA_HEADS * 2 * LANE).astype(BF16),
            rot.reshape(r, MLA_HEADS * MLA_ROPE).astype(BF16))


def kernel(x, c, positions, norm_w, ada_w, ada_b, w_in, mla_q_norm, w_uq, mla_kv_norm, w_ukv,
           gla_w_g2, gla_b_g2, gla_norm, w_out, final_norm):
    b, s, d = x.shape
    depth = w_in.shape[0]
    t_tok = min(512, s)

    mod = _ada_mod(c, ada_w, ada_b).reshape(depth, b, 3, d)

    kv_hd = MLA_NOPE + MLA_V
    q_hd = MLA_NOPE + MLA_ROPE
    w_ukv4 = w_ukv.reshape(depth, MLA_KV_RANK, MLA_HEADS, kv_hd)
    wk_nope = jnp.moveaxis(w_ukv4[..., :MLA_NOPE], 2, 1)
    wuv = jnp.transpose(w_ukv4[..., MLA_NOPE:], (0, 2, 3, 1)).astype(BF16)
    wq_nope = jnp.moveaxis(
        w_uq.reshape(depth, MLA_Q_RANK, MLA_HEADS, q_hd)[..., :MLA_NOPE], 2, 1)
    w_abs = _absorb_weights(wq_nope, wk_nope)

    prev = None
    for l in range(depth):
        w_packed = _pack_w_in(w_in[l])
        wq_all, wrot = _pack_mla_q(w_uq[l], w_abs[l])
        w_g2p = jnp.zeros((LANE, GLA_KWIDTH), F32).at[MLA_ROPE:MLA_ROPE + GLA_GATE_RANK].set(
            gla_w_g2[l]).astype(BF16)

        front = (x, mod[l], norm_w[l].reshape(1, d), w_packed, mla_q_norm[l].reshape(1, MLA_Q_RANK),
                 mla_kv_norm[l].reshape(1, MLA_KV_RANK), wq_all, wrot, t_tok)
        if prev is None:
            x, proj, q_s, k_c, v_t, cos_r, sin_r, cos_m, sin_m = _layer_in(*front, positions=positions)
        else:
            x, proj, q_s, k_c, v_t = _layer_in(*front, tables=(cos_m, sin_m), prev=prev)
        r_o, g_o = _linear_mixers(proj, cos_r, sin_r, w_g2p, gla_b_g2[l].reshape(1, GLA_KWIDTH),
                                  jnp.tile(gla_norm[l], GLA_HEADS).reshape(1, GLA_WIDTH))
        m_o = _mla_attn(q_s, k_c, v_t, proj, wuv[l])
        prev = (r_o, m_o, g_o, mod[l], w_out[l].astype(BF16))
    r_o, m_o, g_o, mod_last, w_out_bf = prev
    t_out = OUT_TILE if s % OUT_TILE == 0 else t_tok
    return _out_proj(r_o, m_o, g_o, x, mod_last, w_out_bf, final_norm.reshape(1, d), t_out)
```

```python
import functools

import numpy as np

import jax
import jax.numpy as jnp
from jax import lax
from jax.experimental import pallas as pl
from jax.experimental.pallas import tpu as pltpu

F32 = jnp.float32
BF16 = jnp.bfloat16

D_MODEL = 1024
CHUNK = 64
EPS = 1e-6
ROPE_THETA = 10000.0

RET_HEADS = 4
RET_HEAD_DIM = 64
RET_WIDTH = 256
MLA_HEADS = 8
MLA_NOPE = 64
MLA_ROPE = 32
MLA_V = 64
MLA_WIDTH = 512
MLA_Q_RANK = 256
MLA_KV_RANK = 128
GLA_HEADS = 4
GLA_DK = 32
GLA_DV = 64
GLA_KWIDTH = 128
GLA_WIDTH = 256
GLA_GATE_RANK = 16
GLA_TAU = 16.0
IN_COLS = 2736

LANE = 128

OFF_MZ = 0
OFF_RQ = 512
OFF_RK = 768
OFF_RV = 1024
OFF_RZ = 1280
OFF_MQ = 1536
OFF_GV = 1792
OFF_GZ = 2048
OFF_KV = 2304
OFF_MB = 2432
MB_ROT_LANE = 48
OFF_GQ = 2560
OFF_GK = 2688
NP_COLS = 2816

MIX_TILE = 256
MIX_SEQS = 4
ATT_T = 256
ATT_SEQS = 4
VT_ROWS = 144
LOG2E = 1.4426950408889634
MASK_NEG = -1e30
OUT_TILE = 1024
VMEM_LIMIT = 56 * 1024 * 1024


def _cparams(sem, flags=None):
    return pltpu.CompilerParams(dimension_semantics=sem, vmem_limit_bytes=VMEM_LIMIT, flags=flags)


def _dot(a, b):
    return jnp.dot(a, b, preferred_element_type=F32)


def _dot_nt(a, b):
    return lax.dot_general(a, b, (((1,), (1,)), ((), ())), preferred_element_type=F32)


def _dot_tn(a, b):
    return lax.dot_general(a, b, (((0,), (0,)), ((), ())), preferred_element_type=F32)


def _silu(x):
    return x / (1.0 + jnp.exp(-x))


def _ada_kernel(c_ref, w_ref, b_ref, o_ref):
    c = c_ref[...]
    o_ref[0] = _dot(_silu(c), w_ref[0]) + b_ref[0]


def _ada_mod(c, ada_w, ada_b):
    depth, d, d3 = ada_w.shape
    b = c.shape[0]
    nblk = d3 // d
    return pl.pallas_call(
        _ada_kernel,
        grid=(depth, nblk),
        in_specs=[
            pl.BlockSpec((b, d), lambda l, j: (0, 0)),
            pl.BlockSpec((1, d, d), lambda l, j: (l, 0, j)),
            pl.BlockSpec((1, 1, d), lambda l, j: (l, 0, j)),
        ],
        out_specs=pl.BlockSpec((1, b, d), lambda l, j: (l, 0, j)),
        out_shape=jax.ShapeDtypeStruct((depth, b, d3), F32),
        compiler_params=_cparams(("arbitrary", "arbitrary")),
        name="ada_mod",
    )(c, ada_w, ada_b.reshape(depth, 1, d3))


def _rope_rows(pos, inv_row):
    half_r = RET_HEAD_DIM // 2
    half_m = MLA_ROPE // 2
    t2 = pos.shape[0] // 2
    lane = lax.broadcasted_iota(jnp.int32, (t2, LANE), 1)
    ang = jnp.where(lane < LANE // 2, pos[:t2], pos[t2:]) * inv_row
    is_r = lane < half_r
    is_m = (lane >= half_r) & (lane < half_r + half_m)

    def tables(x):
        r = jnp.where(is_r, x, 0.0)
        r = r + pltpu.roll(r, half_r, 1)
        r = r + pltpu.roll(r, 2 * half_r, 1)
        m = pltpu.roll(jnp.where(is_m, x, 0.0), LANE - half_r, 1)
        m = m + pltpu.roll(m, half_m, 1)
        return r, m

    def both_halves(x):
        r_lo, m_lo = tables(x)
        r_hi, m_hi = tables(pltpu.roll(x, LANE // 2, 1))
        return jnp.concatenate([r_lo, r_hi], axis=0), jnp.concatenate([m_lo, m_hi], axis=0)

    cos_r, cos_m = both_halves(jnp.cos(ang))
    sin_r, sin_m = both_halves(jnp.sin(ang))
    return cos_r, sin_r, cos_m, sin_m


def _rope_inv_row():
    half_r = RET_HEAD_DIM // 2
    half_m = MLA_ROPE // 2
    inv_r = ROPE_THETA ** (-jnp.arange(half_r, dtype=F32) / half_r)
    inv_m = ROPE_THETA ** (-jnp.arange(half_m, dtype=F32) / half_m)
    one = jnp.concatenate([inv_r, inv_m, jnp.zeros((LANE // 2 - half_r - half_m,), F32)])
    return jnp.tile(one, 2).reshape(1, LANE)


IN_NCHUNK = 1024


def _retention_phases(q_ref, k_ref, v_ref, z_ref, dmat_ref, qw_ref, kw_ref,
                      sdec_ref, bd_ref, ones_ref, o_ref, state_ref):
    nb = q_ref.shape[0]

    lane = lax.broadcasted_iota(jnp.int32, (1, RET_WIDTH), 1)
    qk_head = (lane % LANE) // (RET_HEAD_DIM // 2)
    v_head = lane // RET_HEAD_DIM
    zero_bf = jnp.zeros((), BF16)

    def stack(a, head_of_lane):
        return jnp.concatenate(
            [jnp.where(head_of_lane == h, a, zero_bf) for h in range(RET_HEADS)], axis=0)

    seqs = range(nb)
    q = [q_ref[i] for i in seqs]
    k = [k_ref[i] for i in seqs]
    v = [v_ref[i] for i in seqs]
    state = [state_ref[i] for i in seqs]
    u = [_dot_tn((k[i].astype(F32) * kw_ref[...]).astype(BF16), v[i]) for i in seqs]
    inter = [_dot((q[i].astype(F32) * qw_ref[...]).astype(BF16), state[i].astype(BF16)) for i in seqs]
    yield
    scores = [(_dot_nt(q[i], stack(k[i], qk_head)) * dmat_ref[...]).astype(BF16) for i in seqs]
    for i in seqs:
        state_ref[i] = state[i] * sdec_ref[...] + u[i] * bd_ref[...]
    yield
    intra = [_dot(scores[i], stack(v[i], v_head)) for i in seqs]
    yield
    o = [intra[i] + inter[i] for i in seqs]
    ms = [_dot((o[i] * o[i]).astype(BF16), ones_ref[...]) for i in seqs]
    yield
    for i in seqs:
        y = o[i] * lax.rsqrt(ms[i] + EPS)
        z = z_ref[i].astype(F32)
        o_ref[i] = (y * _silu(z)).astype(BF16)


def _retention_consts(t):
    f32 = F32
    h = RET_HEADS
    log_gamma = jnp.log1p(-jnp.exp2(-5.0 - jnp.arange(h, dtype=f32)))
    idx = jnp.arange(t, dtype=f32)
    dist = jnp.abs(idx[:, None] - idx[None, :])
    ci = jnp.arange(t) // CHUNK
    vis = (ci[None, :] <= ci[:, None]).astype(f32)
    k_scale = RET_HEAD_DIM ** -0.5
    dmat = jnp.exp(log_gamma[:, None, None] * dist[None]) * vis[None] * k_scale
    dmat = jnp.moveaxis(dmat, 0, 1).reshape(t, h * t)
    lane = jnp.arange(RET_WIDTH)
    qk_head = (lane % LANE) // (RET_HEAD_DIM // 2)
    v_head = lane // RET_HEAD_DIM
    lg_lane = log_gamma[qk_head]
    qw = jnp.exp((idx + 1.0)[:, None] * lg_lane[None, :])
    kw = jnp.exp((t - 1.0 - idx)[:, None] * lg_lane[None, :]) * k_scale
    bd = (qk_head[:, None] == v_head[None, :]).astype(f32)
    sdec = jnp.exp(t * lg_lane)[:, None] * bd
    ones = ((v_head[:, None] == v_head[None, :]).astype(f32) / RET_HEAD_DIM).astype(BF16)
    return dmat, qw, kw, sdec, bd, ones


def _gla_phases(q_ref, k_ref, v_ref, z_ref, la_ref, ltri_ref, bdt_ref, ones_ref,
                nw_ref, o_ref, state_ref):
    nb, t = q_ref.shape[0], q_ref.shape[1]
    nchunk = t // CHUNK

    lane_k = lax.broadcasted_iota(jnp.int32, (1, GLA_KWIDTH), 1) // GLA_DK
    lane_v = lax.broadcasted_iota(jnp.int32, (1, GLA_WIDTH), 1) // GLA_DV
    row_i = lax.broadcasted_iota(jnp.int32, (CHUNK, GLA_HEADS * CHUNK), 0)
    col_j = lax.broadcasted_iota(jnp.int32, (CHUNK, GLA_HEADS * CHUNK), 1) % CHUNK
    causal = row_i >= col_j
    zero_bf = jnp.zeros((), BF16)
    ltri = ltri_ref[...]
    bdt = bdt_ref[...]
    ones = ones_ref[...]
    nw = nw_ref[...]

    def stack(a, head_of_lane):
        return jnp.concatenate(
            [jnp.where(head_of_lane == h, a, zero_bf) for h in range(GLA_HEADS)], axis=0)

    seqs = range(nb)
    log_a = [la_ref[i] for i in seqs]
    la_hi = [x.astype(BF16) for x in log_a]
    la_lo = [(x - h.astype(F32)).astype(BF16) for x, h in zip(log_a, la_hi)]
    cum = [_dot(ltri, h) + _dot(ltri, l) for h, l in zip(la_hi, la_lo)]
    yield
    q = [q_ref[i].astype(F32) for i in seqs]
    k = [k_ref[i].astype(F32) * (GLA_DK ** -0.5) for i in seqs]
    e_pos = [jnp.exp(x) for x in cum]
    e_neg = [jnp.exp(-x) for x in cum]
    q_pos = [(a * e).astype(BF16) for a, e in zip(q, e_pos)]
    q_neg = [(a * e).astype(BF16) for a, e in zip(q, e_neg)]
    k_neg = [(a * e).astype(BF16) for a, e in zip(k, e_neg)]
    k_pos = [(a * e).astype(BF16) for a, e in zip(k, e_pos)]

    units = [(i, slice(c * CHUNK, (c + 1) * CHUNK)) for i in seqs for c in range(nchunk)]
    v_c = [v_ref[i, r, :] for i, r in units]
    last = [cum[i][r.stop - 1:r.stop, :] for i, r in units]
    k_st = [(k[i][r] * jnp.exp(l - cum[i][r])).astype(BF16) for (i, r), l in zip(units, last)]
    u_t = [_dot_tn(vc, ks) for vc, ks in zip(v_c, k_st)]
    yield
    past = [_dot_nt(q_pos[i][r], stack(k_neg[i][r], lane_k)) for i, r in units]
    fut = [_dot_nt(q_neg[i][r], stack(k_pos[i][r], lane_k)) for i, r in units]
    attn = [jnp.where(causal, p, f).astype(BF16) for p, f in zip(past, fut)]
    yield
    intra = [_dot(a, stack(vc, lane_v)) for a, vc in zip(attn, v_c)]
    yield
    states = []
    for i in seqs:
        st = state_ref[i]
        for c in range(nchunk):
            states.append(st)
            st = st * jnp.exp(last[i * nchunk + c]) + u_t[i * nchunk + c] * bdt
        state_ref[i] = st
    inter = [_dot_nt(q_pos[i][r], st.astype(BF16)) for (i, r), st in zip(units, states)]
    yield
    o = [a + b for a, b in zip(intra, inter)]
    ms = [_dot((x * x).astype(BF16), ones) for x in o]
    yield
    for (i, r), x, m in zip(units, o, ms):
        y = x * lax.rsqrt(m + EPS) * nw
        z = z_ref[i, r, :].astype(F32)
        o_ref[i, r, :] = (y * _silu(z)).astype(BF16)


N_RET_IN = 10
N_RET_SEQ = 4
N_GLA_IN = 9
N_GLA_SEQ = 5
GROUP_ORDER = "GRGRGGRGGRGR"
MIX_GROUPS = 4
MIX_DELAY = 6


def _mixer_order(n_groups):
    order, step = [], 0
    while True:
        active = [(g, step - g * MIX_DELAY) for g in range(n_groups)
                  if 0 <= step - g * MIX_DELAY < len(GROUP_ORDER)]
        if not active and step >= (n_groups - 1) * MIX_DELAY + len(GROUP_ORDER):
            return order
        order += [f"{GROUP_ORDER[k]}{g}" for g, k in active]
        step += 1


def _linear_mixers_kernel(*refs):
    ret_in = refs[:N_RET_IN]
    gla_in = refs[N_RET_IN:N_RET_IN + N_GLA_IN]
    ret_out, gla_out, ret_state, gla_state = refs[N_RET_IN + N_GLA_IN:]
    nb = ret_out.shape[0]

    @pl.when(pl.program_id(1) == 0)
    def _():
        ret_state[...] = jnp.zeros_like(ret_state)
        gla_state[...] = jnp.zeros_like(gla_state)

    n_groups = MIX_GROUPS if nb % MIX_GROUPS == 0 else 1
    per = nb // n_groups
    phases = {}
    for grp in range(n_groups):
        def sub(r, grp=grp):
            return r.at[pl.ds(grp * per, per)]
        phases[f"R{grp}"] = _retention_phases(
            *[sub(r) for r in ret_in[:N_RET_SEQ]], *ret_in[N_RET_SEQ:], sub(ret_out), sub(ret_state))
        phases[f"G{grp}"] = _gla_phases(
            *[sub(r) for r in gla_in[:N_GLA_SEQ]], *gla_in[N_GLA_SEQ:], sub(gla_out), sub(gla_state))
    for who in _mixer_order(n_groups):
        next(phases[who], None)
    for gen in phases.values():
        assert next(gen, "done") == "done", "the phase order does not cover every phase"


def _linear_mixers(proj, log_a, gla_norm_row):
    b, s, _ = proj.shape
    t = min(MIX_TILE, s)
    nb = MIX_SEQS if b % MIX_SEQS == 0 else 1
    dmat, qw, kw, sdec, bd, r_ones = _retention_consts(t)
    idx = jnp.arange(t)
    ltri = ((idx[:, None] >= idx[None, :]) & (idx[:, None] // CHUNK == idx[None, :] // CHUNK)).astype(BF16)
    kh = jnp.arange(GLA_KWIDTH) // GLA_DK
    vh = jnp.arange(GLA_WIDTH) // GLA_DV
    bdt = (vh[:, None] == kh[None, :]).astype(F32)
    g_ones = ((vh[:, None] == vh[None, :]).astype(F32) / GLA_DV).astype(BF16)

    def col(off, w):
        return pl.BlockSpec((nb, t, w), lambda i, j, o=off // w: (i, j, o))

    def const(a):
        return pl.BlockSpec(a.shape, lambda i, j: (0, 0))

    ret_consts = [dmat, qw, kw, sdec, bd, r_ones]
    gla_consts = [ltri, bdt, g_ones, gla_norm_row]
    in_specs = ([col(OFF_RQ, RET_WIDTH), col(OFF_RK, RET_WIDTH), col(OFF_RV, RET_WIDTH),
                 col(OFF_RZ, RET_WIDTH)] + [const(a) for a in ret_consts]
                + [col(OFF_GQ, GLA_KWIDTH), col(OFF_GK, GLA_KWIDTH), col(OFF_GV, GLA_WIDTH),
                   col(OFF_GZ, GLA_WIDTH), pl.BlockSpec((nb, t, GLA_KWIDTH), lambda i, j: (i, j, 0))]
                + [const(a) for a in gla_consts])
    args = [proj] * 4 + ret_consts + [proj] * 4 + [log_a] + gla_consts
    assert len(in_specs) == N_RET_IN + N_GLA_IN == len(args)
    return pl.pallas_call(
        _linear_mixers_kernel,
        grid=(b // nb, s // t),
        in_specs=in_specs,
        out_specs=[pl.BlockSpec((nb, t, RET_WIDTH), lambda i, j: (i, j, 0)),
                   pl.BlockSpec((nb, t, GLA_WIDTH), lambda i, j: (i, j, 0))],
        out_shape=[jax.ShapeDtypeStruct((b, s, RET_WIDTH), BF16),
                   jax.ShapeDtypeStruct((b, s, GLA_WIDTH), BF16)],
        scratch_shapes=[pltpu.VMEM((nb, RET_WIDTH, RET_WIDTH), F32),
                        pltpu.VMEM((nb, GLA_WIDTH, GLA_KWIDTH), F32)],
        compiler_params=_cparams(("arbitrary", "arbitrary")),
        name="linear_mixers",
    )(*args)


def _absorb_kernel(wq_ref, wk_ref, o_ref):
    o_ref[0, 0] = lax.dot_general(wq_ref[0, 0], wk_ref[0, 0], (((1,), (1,)), ((), ())),
                                  precision=lax.Precision.HIGHEST, preferred_element_type=F32)


def _absorb_weights(wq_nope, wk_nope):
    depth, h, r, dn = wq_nope.shape
    kr = wk_nope.shape[2]
    return pl.pallas_call(
        _absorb_kernel,
        grid=(depth, h),
        in_specs=[pl.BlockSpec((1, 1, r, dn), lambda l, i: (l, i, 0, 0)),
                  pl.BlockSpec((1, 1, kr, dn), lambda l, i: (l, i, 0, 0))],
        out_specs=pl.BlockSpec((1, 1, r, kr), lambda l, i: (l, i, 0, 0)),
        out_shape=jax.ShapeDtypeStruct((depth, h, r, kr), F32),
        compiler_params=_cparams(("arbitrary", "arbitrary")),
        name="mla_absorb",
    )(wq_nope, wk_nope)


def _mla_attn_kernel(q_ref, k_ref, vt_ref, z_ref, wuv_ref, o_ref, m_ref, acc_ref,
                     s0_ref, s1_ref, mt0_ref, mt1_ref):
    nb, tq = q_ref.shape[0], q_ref.shape[2]
    t = ATT_T
    n_tiles = pl.program_id(1) + 1
    units = [(i, h) for h in range(MLA_HEADS) for i in range(nb)]

    def key_start(kt):
        return pl.multiple_of(jnp.minimum(kt, n_tiles - 1) * t, t)

    def scores(i, h, k0, s_ref, mt_ref):
        cols = slice(h * tq, (h + 1) * tq)
        s = _dot_nt(k_ref[i, pl.ds(k0, t), :], q_ref[i, h])
        s_ref[i, :, cols] = s
        mt_ref[i, :, cols] = jnp.max(s, axis=0, keepdims=True)

    def accumulate(i, h, k0, s_ref, mt_ref, first=False):
        cols = slice(h * tq, (h + 1) * tq)
        m_new = mt_ref[i, :, cols]
        if not first:
            m_prev = m_ref[i, :, cols]
            m_new = jnp.maximum(m_prev, m_new)
        p = jnp.exp2(s_ref[i, :, cols] - m_new).astype(BF16)
        pv = _dot(vt_ref[i, :, pl.ds(k0, t)], p)
        if not first:
            pv = jnp.exp2(m_prev - m_new) * acc_ref[i, :, cols] + pv
        acc_ref[i, :, cols] = pv
        m_ref[i, :, cols] = m_new

    def overlapped(k_next, k_cur, s_next, mt_next, s_cur, mt_cur, first=False):
        for i, h in units:
            accumulate(i, h, k_cur, s_cur, mt_cur, first)
            scores(i, h, k_next, s_next, mt_next)

    for i, h in units:
        scores(i, h, 0, s0_ref, mt0_ref)
    overlapped(key_start(1), 0, s1_ref, mt1_ref, s0_ref, mt0_ref, first=True)

    def pair(j, carry):
        k_a, k_b, k_c = key_start(2 * j + 1), key_start(2 * j + 2), key_start(2 * j + 3)
        overlapped(k_b, k_a, s0_ref, mt0_ref, s1_ref, mt1_ref)
        overlapped(k_c, k_b, s1_ref, mt1_ref, s0_ref, mt0_ref)
        return carry

    lax.fori_loop(0, (n_tiles - 1) // 2, pair, 0)

    @pl.when((n_tiles - 1) % 2 == 1)
    def _():
        for i, h in units:
            accumulate(i, h, key_start(n_tiles - 1), s1_ref, mt1_ref)

    o_h = [(acc_ref[i, 0:LANE, h * tq:(h + 1) * tq]
            * (1.0 / acc_ref[i, LANE:LANE + 1, h * tq:(h + 1) * tq])).astype(BF16) for i, h in units]
    o_t = [_dot(wuv_ref[h], o) for (i, h), o in zip(units, o_h)]
    for i in range(nb):
        o_seq = jnp.concatenate([o for (ii, h), o in zip(units, o_t) if ii == i], axis=0)
        z = z_ref[i].astype(F32)
        o_ref[i] = (o_seq.T * _silu(z)).astype(BF16)


def _mla_attn(q_s, k_c, v_t, proj, wuv):
    b, h, s, _ = q_s.shape
    t = ATT_T
    nq = h * t
    vrows = v_t.shape[1]
    nb = ATT_SEQS if b % ATT_SEQS == 0 else 1
    return pl.pallas_call(
        _mla_attn_kernel,
        grid=(b // nb, s // t),
        in_specs=[pl.BlockSpec((nb, h, t, 2 * LANE), lambda i, j: (i, 0, j, 0)),
                  pl.BlockSpec((nb, s, 2 * LANE), lambda i, j: (i, 0, 0)),
                  pl.BlockSpec((nb, vrows, s), lambda i, j: (i, 0, 0)),
                  pl.BlockSpec((nb, t, MLA_WIDTH), lambda i, j: (i, j, OFF_MZ // MLA_WIDTH)),
                  pl.BlockSpec(wuv.shape, lambda i, j: (0, 0, 0))],
        out_specs=pl.BlockSpec((nb, t, MLA_WIDTH), lambda i, j: (i, j, 0)),
        out_shape=jax.ShapeDtypeStruct((b, s, MLA_WIDTH), BF16),
        scratch_shapes=[pltpu.VMEM((nb, 1, nq), F32), pltpu.VMEM((nb, vrows, nq), F32),
                        pltpu.VMEM((nb, t, nq), F32), pltpu.VMEM((nb, t, nq), F32),
                        pltpu.VMEM((nb, 1, nq), F32), pltpu.VMEM((nb, 1, nq), F32)],
        compiler_params=_cparams(("arbitrary", "arbitrary")),
        name="mla_attn",
    )(q_s, k_c, v_t, proj, wuv)


def _layer_in_kernel(*refs, prev, n_chunks):
    if prev:
        (r_ref, m_ref, g_ref, x_ref, modp_ref, wo_ref, mod_ref, nw_ref, w_ref, cos_ref, sin_ref,
         cosr_ref, sinr_ref, qnw_ref, kvnw_ref, wq_ref, wrot_ref, wg_ref, bg_ref,
         xo_ref, p_ref, q_out, k_out, vt_out, la_out) = refs
        y = _dot(r_ref[0], wo_ref[0:RET_WIDTH, :])
        y += _dot(m_ref[0], wo_ref[RET_WIDTH:RET_WIDTH + MLA_WIDTH, :])
        y += _dot(g_ref[0], wo_ref[RET_WIDTH + MLA_WIDTH:, :])
        x = x_ref[0] + modp_ref[0, 2:3, :] * y
        xo_ref[0] = x
        cos, sin, cos_r, sin_r = cos_ref[0], sin_ref[0], cosr_ref[0], sinr_ref[0]
    else:
        (x_ref, pos_ref, inv_ref, mod_ref, nw_ref, w_ref, qnw_ref, kvnw_ref, wq_ref, wrot_ref,
         wg_ref, bg_ref, p_ref, q_out, k_out, vt_out, la_out, cr_out, sr_out, cm_out, sm_out) = refs
        x = x_ref[0]
        cos_r, sin_r, cos, sin = _rope_rows(pos_ref[0].astype(F32), inv_ref[...])
        cr_out[0], sr_out[0], cm_out[0], sm_out[0] = cos_r, sin_r, cos, sin
    t = x.shape[0]
    shift = mod_ref[0, 0:1, :]
    scale = mod_ref[0, 1:2, :]
    hn = x * lax.rsqrt(jnp.mean(x * x, axis=-1, keepdims=True) + EPS) * nw_ref[...]
    act = (hn * (1.0 + scale) + shift).astype(BF16)

    def project(c0):
        c1 = min(c0 + IN_NCHUNK, NP_COLS)
        a = _dot(act, w_ref[:, c0:c1])
        rot0, rot1 = OFF_RQ, OFF_RK + RET_WIDTH
        if c0 <= rot0 and rot1 <= c1:
            if c0 < rot0:
                p_ref[0, :, c0:rot0] = a[:, :rot0 - c0].astype(BF16)
            for off in (OFF_RQ, OFF_RK):
                x1 = a[:, off - c0:off - c0 + LANE]
                x2 = a[:, off - c0 + LANE:off - c0 + 2 * LANE]
                p_ref[0, :, off:off + LANE] = (x1 * cos_r - x2 * sin_r).astype(BF16)
                p_ref[0, :, off + LANE:off + 2 * LANE] = (x2 * cos_r + x1 * sin_r).astype(BF16)
            if rot1 < c1:
                p_ref[0, :, rot1:c1] = a[:, rot1 - c0:].astype(BF16)
        else:
            assert c1 <= rot0 or rot1 <= c0, "a slab must hold both rotary blocks or neither"
            p_ref[0, :, c0:c1] = a.astype(BF16)
        return a

    c_mq = OFF_MQ // IN_NCHUNK * IN_NCHUNK
    c_kv = OFF_KV // IN_NCHUNK * IN_NCHUNK
    assert c_kv == OFF_MB // IN_NCHUNK * IN_NCHUNK and c_mq != c_kv
    slab_q = project(c_mq)
    slab_kv = project(c_kv)
    rest = [c0 for c0 in range(0, NP_COLS, IN_NCHUNK) if c0 not in (c_mq, c_kv)]

    sm_scale = (MLA_NOPE + MLA_ROPE) ** -0.5 * LOG2E
    lat = slab_q[:, OFF_MQ - c_mq:OFF_MQ - c_mq + MLA_Q_RANK]
    lat = lat * lax.rsqrt(jnp.mean(lat * lat, axis=-1, keepdims=True) + EPS) * (qnw_ref[...] * sm_scale)
    lat = lat.astype(BF16)
    tok_chunk = (pl.program_id(1) * t + lax.broadcasted_iota(jnp.int32, (t, LANE), 0)) // CHUNK
    code_lane = lax.broadcasted_iota(jnp.int32, (t, LANE), 1) - MLA_ROPE
    q_code = jnp.where((code_lane > tok_chunk) & (code_lane < n_chunks), MASK_NEG, 0.0)
    k_code = jnp.where(code_lane == tok_chunk, 1.0, 0.0)

    kv = slab_kv[:, OFF_KV - c_kv:OFF_KV - c_kv + MLA_KV_RANK]
    kvn = kv * lax.rsqrt(jnp.mean(kv * kv, axis=-1, keepdims=True) + EPS) * kvnw_ref[...]
    k_out[0, :, :LANE] = kvn.astype(BF16)
    mb = slab_kv[:, OFF_MB - c_kv:OFF_MB - c_kv + LANE]
    mb_rot = pltpu.roll(mb, LANE - MB_ROT_LANE, 1)
    k_out[0, :, LANE:] = (mb * cos + mb_rot * sin + k_code).astype(BF16)
    gate = _dot(mb.astype(BF16), wg_ref[...]) + bg_ref[...]
    la_out[0] = (jnp.minimum(gate, 0.0) - jnp.log(1.0 + jnp.exp(-jnp.abs(gate)))) / GLA_TAU
    vt_out[0, 0:LANE, :] = kvn.T.astype(BF16)
    pad_rows = vt_out.shape[1] - LANE
    row = lax.broadcasted_iota(jnp.int32, (pad_rows, t), 0)
    vt_out[0, LANE:, :] = jnp.where(row == 0, 1.0, 0.0).astype(BF16)

    rot_all = _dot(lat, wrot_ref[...])
    per_blk = LANE // MLA_ROPE
    heads_per_slab = -(-MLA_HEADS // max(len(rest), 1))
    for h in range(MLA_HEADS):
        if h % heads_per_slab == 0 and rest:
            project(rest.pop(0))
        main = _dot(lat, wq_ref[:, 2 * LANE * h:2 * LANE * (h + 1)])
        blk = rot_all[:, LANE * (h // per_blk):LANE * (h // per_blk + 1)]
        lane0 = MLA_ROPE * (h % per_blk)
        rot = blk if lane0 == 0 else pltpu.roll(blk, LANE - lane0, 1)
        q_out[0, h, :, :LANE] = main[:, :LANE].astype(BF16)
        q_out[0, h, :, LANE:] = (main[:, LANE:] * cos + rot * sin + q_code).astype(BF16)
    for c0 in rest:
        project(c0)


def _layer_in(x, mod3, norm_w_row, w_packed, qn_row, kvn_row, wq_all, wrot, w_g2p, b_g2_row, t, *,
              positions=None, tables=None, prev=None):
    b, s, d = x.shape
    n_chunks = s // CHUNK
    assert MLA_ROPE + n_chunks <= LANE, "chunk-mask code does not fit the spare contraction lanes"
    first = prev is None
    assert first == (positions is not None) and first == (tables is None)

    def tok(w):
        return pl.BlockSpec((1, t, w), lambda i, j: (i, j, 0))

    def const(a):
        return pl.BlockSpec(a.shape, lambda i, j: (0,) * a.ndim)

    mod_spec = pl.BlockSpec((1, 3, d), lambda i, j: (i, 0, 0))
    common_specs = [mod_spec, const(norm_w_row), const(w_packed)]
    common_args = [mod3, norm_w_row, w_packed]
    tail_specs = [const(qn_row), const(kvn_row), const(wq_all), const(wrot), const(w_g2p), const(b_g2_row)]
    tail_args = [qn_row, kvn_row, wq_all, wrot, w_g2p, b_g2_row]
    out_specs = [tok(NP_COLS),
                 pl.BlockSpec((1, MLA_HEADS, t, 2 * LANE), lambda i, j: (i, 0, j, 0)),
                 tok(2 * LANE),
                 pl.BlockSpec((1, VT_ROWS, t), lambda i, j: (i, 0, j)),
                 tok(GLA_KWIDTH)]
    out_shape = [jax.ShapeDtypeStruct((b, s, NP_COLS), BF16),
                 jax.ShapeDtypeStruct((b, MLA_HEADS, s, 2 * LANE), BF16),
                 jax.ShapeDtypeStruct((b, s, 2 * LANE), BF16),
                 jax.ShapeDtypeStruct((b, VT_ROWS, s), BF16),
                 jax.ShapeDtypeStruct((b, s, GLA_KWIDTH), F32)]
    if first:
        inv_row = _rope_inv_row()
        in_specs = [tok(d), tok(1), const(inv_row)] + common_specs + tail_specs
        args = [x, positions.reshape(b, s, 1), inv_row] + common_args + tail_args
        out_specs = out_specs + [tok(LANE)] * 4
        out_shape = out_shape + [jax.ShapeDtypeStruct((b, s, LANE), F32)] * 4
    else:
        r_o, m_o, g_o, mod3_prev, w_out_bf = prev
        in_specs = ([tok(RET_WIDTH), tok(MLA_WIDTH), tok(GLA_WIDTH), tok(d), mod_spec, const(w_out_bf)]
                    + common_specs + [tok(LANE)] * len(tables) + tail_specs)
        args = [r_o, m_o, g_o, x, mod3_prev, w_out_bf] + common_args + list(tables) + tail_args
        out_specs = [tok(d)] + out_specs
        out_shape = [jax.ShapeDtypeStruct((b, s, d), F32)] + out_shape
    outs = pl.pallas_call(
        functools.partial(_layer_in_kernel, prev=not first, n_chunks=n_chunks),
        grid=(b, s // t),
        in_specs=in_specs,
        out_specs=out_specs,
        out_shape=out_shape,
        compiler_params=_cparams(("arbitrary", "arbitrary")),
        name="layer_in",
    )(*args)
    return [x] + list(outs) if first else list(outs)


def _out_proj_kernel(r_ref, m_ref, g_ref, x_ref, mod_ref, w_ref, fw_ref, o_ref):
    y = _dot(r_ref[0], w_ref[0:RET_WIDTH, :])
    y += _dot(m_ref[0], w_ref[RET_WIDTH:RET_WIDTH + MLA_WIDTH, :])
    y += _dot(g_ref[0], w_ref[RET_WIDTH + MLA_WIDTH:, :])
    gate = mod_ref[0, 2:3, :]
    x = x_ref[0] + gate * y
    o_ref[0] = x * lax.rsqrt(jnp.mean(x * x, axis=-1, keepdims=True) + EPS) * fw_ref[...]


def _out_proj(r_o, m_o, g_o, x, mod3, w_out_bf, final_row, t):
    b, s, d = x.shape

    def tok(w):
        return pl.BlockSpec((1, t, w), lambda i, j: (i, j, 0))

    return pl.pallas_call(
        _out_proj_kernel,
        grid=(b, s // t),
        in_specs=[tok(RET_WIDTH), tok(MLA_WIDTH), tok(GLA_WIDTH), tok(d),
                  pl.BlockSpec((1, 3, d), lambda i, j: (i, 0, 0)),
                  pl.BlockSpec(w_out_bf.shape, lambda i, j: (0, 0)),
                  pl.BlockSpec((1, d), lambda i, j: (0, 0))],
        out_specs=tok(d),
        out_shape=jax.ShapeDtypeStruct((b, s, d), F32),
        compiler_params=_cparams(("arbitrary", "arbitrary")),
        name="out_proj",
    )(r_o, m_o, g_o, x, mod3, w_out_bf, final_row)


def _pack_w_in(w_in_l):
    d = w_in_l.shape[0]
    cuts = np.cumsum([RET_WIDTH] * 4 + [MLA_Q_RANK, MLA_KV_RANK, MLA_ROPE, MLA_WIDTH]
                     + [GLA_KWIDTH, GLA_KWIDTH, GLA_WIDTH, GLA_GATE_RANK, GLA_WIDTH])[:-1]
    rq, rk, rv, rz, mq, kv, kr, mz, gq, gk, gv, gg, gz = jnp.split(w_in_l, [int(c) for c in cuts], axis=1)

    def halves_first(w):
        w4 = w.reshape(d, RET_HEADS, 2, RET_HEAD_DIM // 2)
        return jnp.swapaxes(w4, 1, 2).reshape(d, RET_WIDTH)

    hm = MLA_ROPE // 2
    kr_rot = jnp.concatenate([-kr[:, hm:], kr[:, :hm]], axis=1)
    mb_pad = jnp.zeros((d, LANE - MB_ROT_LANE - MLA_ROPE), F32)
    assert MB_ROT_LANE == MLA_ROPE + GLA_GATE_RANK
    packed = {
        OFF_MZ: mz, OFF_RQ: halves_first(rq), OFF_RK: halves_first(rk), OFF_RV: rv, OFF_RZ: rz,
        OFF_MQ: mq, OFF_GV: gv, OFF_GZ: gz, OFF_KV: kv,
        OFF_MB: jnp.concatenate([kr, gg, kr_rot, mb_pad], axis=1), OFF_GQ: gq, OFF_GK: gk,
    }
    pieces, pos = [], 0
    for off in sorted(packed):
        assert off == pos, "packed in-projection layout has a gap or overlap"
        pieces.append(packed[off])
        pos += packed[off].shape[1]
    assert pos == NP_COLS
    return jnp.concatenate(pieces, axis=1).astype(BF16)


def _pack_mla_q(w_uq_l, w_abs_l):
    r = w_uq_l.shape[0]
    hd = MLA_NOPE + MLA_ROPE
    hm = MLA_ROPE // 2
    w3 = w_uq_l.reshape(r, MLA_HEADS, hd)
    pe = w3[:, :, MLA_NOPE:]
    pad = jnp.zeros((r, MLA_HEADS, LANE - MLA_ROPE), F32)
    main = jnp.concatenate([jnp.moveaxis(w_abs_l, 0, 1), pe, pad], axis=-1)
    rot = jnp.concatenate([-pe[:, :, hm:], pe[:, :, :hm]], axis=-1)
    return (main.reshape(r, MLA_HEADS * 2 * LANE).astype(BF16),
            rot.reshape(r, MLA_HEADS * MLA_ROPE).astype(BF16))


def kernel(x, c, positions, norm_w, ada_w, ada_b, w_in, mla_q_norm, w_uq, mla_kv_norm, w_ukv,
           gla_w_g2, gla_b_g2, gla_norm, w_out, final_norm):
    b, s, d = x.shape
    depth = w_in.shape[0]
    t_tok = min(512, s)

    mod = _ada_mod(c, ada_w, ada_b).reshape(depth, b, 3, d)

    kv_hd = MLA_NOPE + MLA_V
    q_hd = MLA_NOPE + MLA_ROPE
    w_ukv4 = w_ukv.reshape(depth, MLA_KV_RANK, MLA_HEADS, kv_hd)
    wk_nope = jnp.moveaxis(w_ukv4[..., :MLA_NOPE], 2, 1)
    wuv = jnp.transpose(w_ukv4[..., MLA_NOPE:], (0, 2, 3, 1)).astype(BF16)
    wq_nope = jnp.moveaxis(
        w_uq.reshape(depth, MLA_Q_RANK, MLA_HEADS, q_hd)[..., :MLA_NOPE], 2, 1)
    w_abs = _absorb_weights(wq_nope, wk_nope)

    prev = None
    for l in range(depth):
        w_packed = _pack_w_in(w_in[l])
        wq_all, wrot = _pack_mla_q(w_uq[l], w_abs[l])
        w_g2p = jnp.zeros((LANE, GLA_KWIDTH), F32).at[MLA_ROPE:MLA_ROPE + GLA_GATE_RANK].set(
            gla_w_g2[l]).astype(BF16)

        front = (x, mod[l], norm_w[l].reshape(1, d), w_packed, mla_q_norm[l].reshape(1, MLA_Q_RANK),
                 mla_kv_norm[l].reshape(1, MLA_KV_RANK), wq_all, wrot, w_g2p,
                 gla_b_g2[l].reshape(1, GLA_KWIDTH), t_tok)
        if prev is None:
            x, proj, q_s, k_c, v_t, log_a, cos_r, sin_r, cos_m, sin_m = _layer_in(
                *front, positions=positions)
        else:
            x, proj, q_s, k_c, v_t, log_a = _layer_in(
                *front, tables=(cos_m, sin_m, cos_r, sin_r), prev=prev)
        r_o, g_o = _linear_mixers(proj, log_a, jnp.tile(gla_norm[l], GLA_HEADS).reshape(1, GLA_WIDTH))
        m_o = _mla_attn(q_s, k_c, v_t, proj, wuv[l])
        prev = (r_o, m_o, g_o, mod[l], w_out[l].astype(BF16))
    r_o, m_o, g_o, mod_last, w_out_bf = prev
    t_out = OUT_TILE if s % OUT_TILE == 0 else t_tok
    return _out_proj(r_o, m_o, g_o, x, mod_last, w_out_bf, final_norm.reshape(1, d), t_out)
```

```python
import functools

import numpy as np

import jax
import jax.numpy as jnp
from jax import lax
from jax.experimental import pallas as pl
from jax.experimental.pallas import tpu as pltpu

F32 = jnp.float32
BF16 = jnp.bfloat16

D_MODEL = 1024
CHUNK = 64
EPS = 1e-6
ROPE_THETA = 10000.0

RET_HEADS = 4
RET_HEAD_DIM = 64
RET_WIDTH = 256
MLA_HEADS = 8
MLA_NOPE = 64
MLA_ROPE = 32
MLA_V = 64
MLA_WIDTH = 512
MLA_Q_RANK = 256
MLA_KV_RANK = 128
GLA_HEADS = 4
GLA_DK = 32
GLA_DV = 64
GLA_KWIDTH = 128
GLA_WIDTH = 256
GLA_GATE_RANK = 16
GLA_TAU = 16.0
IN_COLS = 2736

LANE = 128

OFF_MZ = 0
OFF_RQ = 512
OFF_RK = 768
OFF_RV = 1024
OFF_RZ = 1280
OFF_MQ = 1536
OFF_GV = 1792
OFF_GZ = 2048
OFF_KV = 2304
OFF_MB = 2432
MB_ROT_LANE = 48
OFF_GQ = 2560
OFF_GK = 2688
NP_COLS = 2816

MIX_TILE = 256
MIX_SEQS = 4
ATT_T = 256
ATT_SEQS = 4
VT_ROWS = 144
LOG2E = 1.4426950408889634
MASK_NEG = -1e30
OUT_TILE = 1024
VMEM_LIMIT = 56 * 1024 * 1024


def _cparams(sem, flags=None):
    return pltpu.CompilerParams(dimension_semantics=sem, vmem_limit_bytes=VMEM_LIMIT, flags=flags)


def _dot(a, b):
    return jnp.dot(a, b, preferred_element_type=F32)


def _dot_nt(a, b):
    return lax.dot_general(a, b, (((1,), (1,)), ((), ())), preferred_element_type=F32)


def _dot_tn(a, b):
    return lax.dot_general(a, b, (((0,), (0,)), ((), ())), preferred_element_type=F32)


def _silu(x):
    return x / (1.0 + jnp.exp(-x))


def _ada_kernel(c_ref, w_ref, b_ref, o_ref):
    c = c_ref[...]
    o_ref[0] = _dot(_silu(c), w_ref[0]) + b_ref[0]


def _ada_mod(c, ada_w, ada_b):
    depth, d, d3 = ada_w.shape
    b = c.shape[0]
    nblk = d3 // d
    return pl.pallas_call(
        _ada_kernel,
        grid=(depth, nblk),
        in_specs=[
            pl.BlockSpec((b, d), lambda l, j: (0, 0)),
            pl.BlockSpec((1, d, d), lambda l, j: (l, 0, j)),
            pl.BlockSpec((1, 1, d), lambda l, j: (l, 0, j)),
        ],
        out_specs=pl.BlockSpec((1, b, d), lambda l, j: (l, 0, j)),
        out_shape=jax.ShapeDtypeStruct((depth, b, d3), F32),
        compiler_params=_cparams(("arbitrary", "arbitrary")),
        name="ada_mod",
    )(c, ada_w, ada_b.reshape(depth, 1, d3))


def _rope_rows(pos, inv_row):
    half_r = RET_HEAD_DIM // 2
    half_m = MLA_ROPE // 2
    t2 = pos.shape[0] // 2
    lane = lax.broadcasted_iota(jnp.int32, (t2, LANE), 1)
    ang = jnp.where(lane < LANE // 2, pos[:t2], pos[t2:]) * inv_row
    is_r = lane < half_r
    is_m = (lane >= half_r) & (lane < half_r + half_m)

    def tables(x):
        r = jnp.where(is_r, x, 0.0)
        r = r + pltpu.roll(r, half_r, 1)
        r = r + pltpu.roll(r, 2 * half_r, 1)
        m = pltpu.roll(jnp.where(is_m, x, 0.0), LANE - half_r, 1)
        m = m + pltpu.roll(m, half_m, 1)
        return r, m

    def both_halves(x):
        r_lo, m_lo = tables(x)
        r_hi, m_hi = tables(pltpu.roll(x, LANE // 2, 1))
        return jnp.concatenate([r_lo, r_hi], axis=0), jnp.concatenate([m_lo, m_hi], axis=0)

    cos_r, cos_m = both_halves(jnp.cos(ang))
    sin_r, sin_m = both_halves(jnp.sin(ang))
    return cos_r, sin_r, cos_m, sin_m


def _rope_inv_row():
    half_r = RET_HEAD_DIM // 2
    half_m = MLA_ROPE // 2
    inv_r = ROPE_THETA ** (-jnp.arange(half_r, dtype=F32) / half_r)
    inv_m = ROPE_THETA ** (-jnp.arange(half_m, dtype=F32) / half_m)
    one = jnp.concatenate([inv_r, inv_m, jnp.zeros((LANE // 2 - half_r - half_m,), F32)])
    return jnp.tile(one, 2).reshape(1, LANE)


IN_NCHUNK = 1024


def _retention_phases(q_ref, k_ref, v_ref, z_ref, dmat_ref, qw_ref, kw_ref,
                      sdec_ref, bd_ref, ones_ref, o_ref, state_ref):
    nb = q_ref.shape[0]

    lane = lax.broadcasted_iota(jnp.int32, (1, RET_WIDTH), 1)
    qk_head = (lane % LANE) // (RET_HEAD_DIM // 2)
    v_head = lane // RET_HEAD_DIM
    zero_bf = jnp.zeros((), BF16)

    def stack(a, head_of_lane):
        return jnp.concatenate(
            [jnp.where(head_of_lane == h, a, zero_bf) for h in range(RET_HEADS)], axis=0)

    seqs = range(nb)
    q = [q_ref[i] for i in seqs]
    k = [k_ref[i] for i in seqs]
    v = [v_ref[i] for i in seqs]
    state = [state_ref[i] for i in seqs]
    u = [_dot_tn((k[i].astype(F32) * kw_ref[...]).astype(BF16), v[i]) for i in seqs]
    inter = [_dot((q[i].astype(F32) * qw_ref[...]).astype(BF16), state[i].astype(BF16)) for i in seqs]
    yield
    scores = [(_dot_nt(q[i], stack(k[i], qk_head)) * dmat_ref[...]).astype(BF16) for i in seqs]
    for i in seqs:
        state_ref[i] = state[i] * sdec_ref[...] + u[i] * bd_ref[...]
    yield
    intra = [_dot(scores[i], stack(v[i], v_head)) for i in seqs]
    yield
    o = [intra[i] + inter[i] for i in seqs]
    ms = [_dot((o[i] * o[i]).astype(BF16), ones_ref[...]) for i in seqs]
    yield
    for i in seqs:
        y = o[i] * lax.rsqrt(ms[i] + EPS)
        o_ref[i] = (y * z_ref[i].astype(F32)).astype(BF16)


def _retention_consts(t):
    f32 = F32
    h = RET_HEADS
    log_gamma = jnp.log1p(-jnp.exp2(-5.0 - jnp.arange(h, dtype=f32)))
    idx = jnp.arange(t, dtype=f32)
    dist = jnp.abs(idx[:, None] - idx[None, :])
    ci = jnp.arange(t) // CHUNK
    vis = (ci[None, :] <= ci[:, None]).astype(f32)
    k_scale = RET_HEAD_DIM ** -0.5
    dmat = jnp.exp(log_gamma[:, None, None] * dist[None]) * vis[None] * k_scale
    dmat = jnp.moveaxis(dmat, 0, 1).reshape(t, h * t)
    lane = jnp.arange(RET_WIDTH)
    qk_head = (lane % LANE) // (RET_HEAD_DIM // 2)
    v_head = lane // RET_HEAD_DIM
    lg_lane = log_gamma[qk_head]
    qw = jnp.exp((idx + 1.0)[:, None] * lg_lane[None, :])
    kw = jnp.exp((t - 1.0 - idx)[:, None] * lg_lane[None, :]) * k_scale
    bd = (qk_head[:, None] == v_head[None, :]).astype(f32)
    sdec = jnp.exp(t * lg_lane)[:, None] * bd
    ones = ((v_head[:, None] == v_head[None, :]).astype(f32) / RET_HEAD_DIM).astype(BF16)
    return dmat, qw, kw, sdec, bd, ones


def _gla_phases(q_ref, k_ref, v_ref, z_ref, la_ref, ltri_ref, bdt_ref, ones_ref,
                nw_ref, o_ref, state_ref):
    nb, t = q_ref.shape[0], q_ref.shape[1]
    nchunk = t // CHUNK

    lane_k = lax.broadcasted_iota(jnp.int32, (1, GLA_KWIDTH), 1) // GLA_DK
    lane_v = lax.broadcasted_iota(jnp.int32, (1, GLA_WIDTH), 1) // GLA_DV
    row_i = lax.broadcasted_iota(jnp.int32, (CHUNK, GLA_HEADS * CHUNK), 0)
    col_j = lax.broadcasted_iota(jnp.int32, (CHUNK, GLA_HEADS * CHUNK), 1) % CHUNK
    causal = row_i >= col_j
    zero_bf = jnp.zeros((), BF16)
    ltri = ltri_ref[...]
    bdt = bdt_ref[...]
    ones = ones_ref[...]
    nw = nw_ref[...]

    def stack(a, head_of_lane):
        return jnp.concatenate(
            [jnp.where(head_of_lane == h, a, zero_bf) for h in range(GLA_HEADS)], axis=0)

    seqs = range(nb)
    log_a = [la_ref[i] for i in seqs]
    la_hi = [x.astype(BF16) for x in log_a]
    la_lo = [(x - h.astype(F32)).astype(BF16) for x, h in zip(log_a, la_hi)]
    cum = [_dot(ltri, h) + _dot(ltri, l) for h, l in zip(la_hi, la_lo)]
    yield
    q = [q_ref[i].astype(F32) for i in seqs]
    k = [k_ref[i].astype(F32) * (GLA_DK ** -0.5) for i in seqs]
    e_pos = [jnp.exp(x) for x in cum]
    e_neg = [jnp.exp(-x) for x in cum]
    q_pos = [(a * e).astype(BF16) for a, e in zip(q, e_pos)]
    q_neg = [(a * e).astype(BF16) for a, e in zip(q, e_neg)]
    k_neg = [(a * e).astype(BF16) for a, e in zip(k, e_neg)]
    k_pos = [(a * e).astype(BF16) for a, e in zip(k, e_pos)]

    units = [(i, slice(c * CHUNK, (c + 1) * CHUNK)) for i in seqs for c in range(nchunk)]
    v_c = [v_ref[i, r, :] for i, r in units]
    last = [cum[i][r.stop - 1:r.stop, :] for i, r in units]
    k_st = [(k[i][r] * jnp.exp(l - cum[i][r])).astype(BF16) for (i, r), l in zip(units, last)]
    u_t = [_dot_tn(vc, ks) for vc, ks in zip(v_c, k_st)]
    yield
    past = [_dot_nt(q_pos[i][r], stack(k_neg[i][r], lane_k)) for i, r in units]
    fut = [_dot_nt(q_neg[i][r], stack(k_pos[i][r], lane_k)) for i, r in units]
    attn = [jnp.where(causal, p, f).astype(BF16) for p, f in zip(past, fut)]
    yield
    intra = [_dot(a, stack(vc, lane_v)) for a, vc in zip(attn, v_c)]
    yield
    states = []
    for i in seqs:
        st = state_ref[i]
        for c in range(nchunk):
            states.append(st)
            st = st * jnp.exp(last[i * nchunk + c]) + u_t[i * nchunk + c] * bdt
        state_ref[i] = st
    inter = [_dot_nt(q_pos[i][r], st.astype(BF16)) for (i, r), st in zip(units, states)]
    yield
    o = [a + b for a, b in zip(intra, inter)]
    ms = [_dot((x * x).astype(BF16), ones) for x in o]
    yield
    for (i, r), x, m in zip(units, o, ms):
        y = x * lax.rsqrt(m + EPS) * nw
        o_ref[i, r, :] = (y * z_ref[i, r, :].astype(F32)).astype(BF16)


N_RET_IN = 10
N_RET_SEQ = 4
N_GLA_IN = 9
N_GLA_SEQ = 5
GROUP_ORDER = "GRGRGGRGGRGR"
MIX_GROUPS = 4
MIX_DELAY = 6


def _mixer_order(n_groups):
    order, step = [], 0
    while True:
        active = [(g, step - g * MIX_DELAY) for g in range(n_groups)
                  if 0 <= step - g * MIX_DELAY < len(GROUP_ORDER)]
        if not active and step >= (n_groups - 1) * MIX_DELAY + len(GROUP_ORDER):
            return order
        order += [f"{GROUP_ORDER[k]}{g}" for g, k in active]
        step += 1


def _linear_mixers_kernel(*refs):
    ret_in = refs[:N_RET_IN]
    gla_in = refs[N_RET_IN:N_RET_IN + N_GLA_IN]
    ret_out, gla_out, ret_state, gla_state = refs[N_RET_IN + N_GLA_IN:]
    nb = ret_out.shape[0]

    @pl.when(pl.program_id(1) == 0)
    def _():
        ret_state[...] = jnp.zeros_like(ret_state)
        gla_state[...] = jnp.zeros_like(gla_state)

    n_groups = MIX_GROUPS if nb % MIX_GROUPS == 0 else 1
    per = nb // n_groups
    phases = {}
    for grp in range(n_groups):
        def sub(r, grp=grp):
            return r.at[pl.ds(grp * per, per)]
        phases[f"R{grp}"] = _retention_phases(
            *[sub(r) for r in ret_in[:N_RET_SEQ]], *ret_in[N_RET_SEQ:], sub(ret_out), sub(ret_state))
        phases[f"G{grp}"] = _gla_phases(
            *[sub(r) for r in gla_in[:N_GLA_SEQ]], *gla_in[N_GLA_SEQ:], sub(gla_out), sub(gla_state))
    for who in _mixer_order(n_groups):
        next(phases[who], None)
    for gen in phases.values():
        assert next(gen, "done") == "done", "the phase order does not cover every phase"


def _linear_mixers(proj, log_a, gla_norm_row):
    b, s, _ = proj.shape
    t = min(MIX_TILE, s)
    nb = MIX_SEQS if b % MIX_SEQS == 0 else 1
    dmat, qw, kw, sdec, bd, r_ones = _retention_consts(t)
    idx = jnp.arange(t)
    ltri = ((idx[:, None] >= idx[None, :]) & (idx[:, None] // CHUNK == idx[None, :] // CHUNK)).astype(BF16)
    kh = jnp.arange(GLA_KWIDTH) // GLA_DK
    vh = jnp.arange(GLA_WIDTH) // GLA_DV
    bdt = (vh[:, None] == kh[None, :]).astype(F32)
    g_ones = ((vh[:, None] == vh[None, :]).astype(F32) / GLA_DV).astype(BF16)

    def col(off, w):
        return pl.BlockSpec((nb, t, w), lambda i, j, o=off // w: (i, j, o))

    def const(a):
        return pl.BlockSpec(a.shape, lambda i, j: (0, 0))

    ret_consts = [dmat, qw, kw, sdec, bd, r_ones]
    gla_consts = [ltri, bdt, g_ones, gla_norm_row]
    in_specs = ([col(OFF_RQ, RET_WIDTH), col(OFF_RK, RET_WIDTH), col(OFF_RV, RET_WIDTH),
                 col(OFF_RZ, RET_WIDTH)] + [const(a) for a in ret_consts]
                + [col(OFF_GQ, GLA_KWIDTH), col(OFF_GK, GLA_KWIDTH), col(OFF_GV, GLA_WIDTH),
                   col(OFF_GZ, GLA_WIDTH), pl.BlockSpec((nb, t, GLA_KWIDTH), lambda i, j: (i, j, 0))]
                + [const(a) for a in gla_consts])
    args = [proj] * 4 + ret_consts + [proj] * 4 + [log_a] + gla_consts
    assert len(in_specs) == N_RET_IN + N_GLA_IN == len(args)
    return pl.pallas_call(
        _linear_mixers_kernel,
        grid=(b // nb, s // t),
        in_specs=in_specs,
        out_specs=[pl.BlockSpec((nb, t, RET_WIDTH), lambda i, j: (i, j, 0)),
                   pl.BlockSpec((nb, t, GLA_WIDTH), lambda i, j: (i, j, 0))],
        out_shape=[jax.ShapeDtypeStruct((b, s, RET_WIDTH), BF16),
                   jax.ShapeDtypeStruct((b, s, GLA_WIDTH), BF16)],
        scratch_shapes=[pltpu.VMEM((nb, RET_WIDTH, RET_WIDTH), F32),
                        pltpu.VMEM((nb, GLA_WIDTH, GLA_KWIDTH), F32)],
        compiler_params=_cparams(("arbitrary", "arbitrary")),
        name="linear_mixers",
    )(*args)


def _absorb_kernel(wq_ref, wk_ref, o_ref):
    o_ref[0, 0] = lax.dot_general(wq_ref[0, 0], wk_ref[0, 0], (((1,), (1,)), ((), ())),
                                  precision=lax.Precision.HIGHEST, preferred_element_type=F32)


def _absorb_weights(wq_nope, wk_nope):
    depth, h, r, dn = wq_nope.shape
    kr = wk_nope.shape[2]
    return pl.pallas_call(
        _absorb_kernel,
        grid=(depth, h),
        in_specs=[pl.BlockSpec((1, 1, r, dn), lambda l, i: (l, i, 0, 0)),
                  pl.BlockSpec((1, 1, kr, dn), lambda l, i: (l, i, 0, 0))],
        out_specs=pl.BlockSpec((1, 1, r, kr), lambda l, i: (l, i, 0, 0)),
        out_shape=jax.ShapeDtypeStruct((depth, h, r, kr), F32),
        compiler_params=_cparams(("arbitrary", "arbitrary")),
        name="mla_absorb",
    )(wq_nope, wk_nope)


def _mla_attn_kernel(q_ref, k_ref, vt_ref, z_ref, wuv_ref, o_ref, m_ref, acc_ref,
                     s0_ref, s1_ref, mt0_ref, mt1_ref):
    nb, tq = q_ref.shape[0], q_ref.shape[2]
    t = ATT_T
    n_tiles = pl.program_id(1) + 1
    units = [(i, h) for h in range(MLA_HEADS) for i in range(nb)]

    def key_start(kt):
        return pl.multiple_of(jnp.minimum(kt, n_tiles - 1) * t, t)

    def scores(i, h, k0, s_ref, mt_ref):
        cols = slice(h * tq, (h + 1) * tq)
        s = _dot_nt(k_ref[i, pl.ds(k0, t), :], q_ref[i, h])
        s_ref[i, :, cols] = s
        mt_ref[i, :, cols] = jnp.max(s, axis=0, keepdims=True)

    def accumulate(i, h, k0, s_ref, mt_ref, first=False):
        cols = slice(h * tq, (h + 1) * tq)
        m_new = mt_ref[i, :, cols]
        if not first:
            m_prev = m_ref[i, :, cols]
            m_new = jnp.maximum(m_prev, m_new)
        p = jnp.exp2(s_ref[i, :, cols] - m_new).astype(BF16)
        pv = _dot(vt_ref[i, :, pl.ds(k0, t)], p)
        if not first:
            pv = jnp.exp2(m_prev - m_new) * acc_ref[i, :, cols] + pv
        acc_ref[i, :, cols] = pv
        m_ref[i, :, cols] = m_new

    def overlapped(k_next, k_cur, s_next, mt_next, s_cur, mt_cur, first=False):
        for i, h in units:
            accumulate(i, h, k_cur, s_cur, mt_cur, first)
            scores(i, h, k_next, s_next, mt_next)

    for i, h in units:
        scores(i, h, 0, s0_ref, mt0_ref)
    overlapped(key_start(1), 0, s1_ref, mt1_ref, s0_ref, mt0_ref, first=True)

    def pair(j, carry):
        k_a, k_b, k_c = key_start(2 * j + 1), key_start(2 * j + 2), key_start(2 * j + 3)
        overlapped(k_b, k_a, s0_ref, mt0_ref, s1_ref, mt1_ref)
        overlapped(k_c, k_b, s1_ref, mt1_ref, s0_ref, mt0_ref)
        return carry

    lax.fori_loop(0, (n_tiles - 1) // 2, pair, 0)

    @pl.when((n_tiles - 1) % 2 == 1)
    def _():
        for i, h in units:
            accumulate(i, h, key_start(n_tiles - 1), s1_ref, mt1_ref)

    o_h = [(acc_ref[i, 0:LANE, h * tq:(h + 1) * tq]
            * (1.0 / acc_ref[i, LANE:LANE + 1, h * tq:(h + 1) * tq])).astype(BF16) for i, h in units]
    o_t = [_dot(wuv_ref[h], o) for (i, h), o in zip(units, o_h)]
    for i in range(nb):
        o_seq = jnp.concatenate([o for (ii, h), o in zip(units, o_t) if ii == i], axis=0)
        o_ref[i] = (o_seq.T * z_ref[i].astype(F32)).astype(BF16)


def _mla_attn(q_s, k_c, v_t, proj, wuv):
    b, h, s, _ = q_s.shape
    t = ATT_T
    nq = h * t
    vrows = v_t.shape[1]
    nb = ATT_SEQS if b % ATT_SEQS == 0 else 1
    return pl.pallas_call(
        _mla_attn_kernel,
        grid=(b // nb, s // t),
        in_specs=[pl.BlockSpec((nb, h, t, 2 * LANE), lambda i, j: (i, 0, j, 0)),
                  pl.BlockSpec((nb, s, 2 * LANE), lambda i, j: (i, 0, 0)),
                  pl.BlockSpec((nb, vrows, s), lambda i, j: (i, 0, 0)),
                  pl.BlockSpec((nb, t, MLA_WIDTH), lambda i, j: (i, j, OFF_MZ // MLA_WIDTH)),
                  pl.BlockSpec(wuv.shape, lambda i, j: (0, 0, 0))],
        out_specs=pl.BlockSpec((nb, t, MLA_WIDTH), lambda i, j: (i, j, 0)),
        out_shape=jax.ShapeDtypeStruct((b, s, MLA_WIDTH), BF16),
        scratch_shapes=[pltpu.VMEM((nb, 1, nq), F32), pltpu.VMEM((nb, vrows, nq), F32),
                        pltpu.VMEM((nb, t, nq), F32), pltpu.VMEM((nb, t, nq), F32),
                        pltpu.VMEM((nb, 1, nq), F32), pltpu.VMEM((nb, 1, nq), F32)],
        compiler_params=_cparams(("arbitrary", "arbitrary")),
        name="mla_attn",
    )(q_s, k_c, v_t, proj, wuv)


def _slab_segments(c0, c1):
    special = sorted([(OFF_RQ, OFF_RQ + RET_WIDTH, "rot"), (OFF_RK, OFF_RK + RET_WIDTH, "rot"),
                      (OFF_MZ, OFF_MZ + MLA_WIDTH, "gate"), (OFF_RZ, OFF_RZ + RET_WIDTH, "gate"),
                      (OFF_GZ, OFF_GZ + GLA_WIDTH, "gate")])
    out, pos = [], c0
    for lo, hi, kind in special:
        if hi <= c0 or c1 <= lo:
            continue
        assert c0 <= lo and hi <= c1, "a special column block must lie inside one slab"
        if pos < lo:
            out.append((pos, lo, "plain"))
        out.append((lo, hi, kind))
        pos = hi
    if pos < c1:
        out.append((pos, c1, "plain"))
    return out


def _layer_in_kernel(*refs, prev, n_chunks):
    if prev:
        (r_ref, m_ref, g_ref, x_ref, modp_ref, wo_ref, mod_ref, nw_ref, w_ref, cos_ref, sin_ref,
         cosr_ref, sinr_ref, qnw_ref, kvnw_ref, wq_ref, wrot_ref, wg_ref, bg_ref,
         xo_ref, p_ref, q_out, k_out, vt_out, la_out) = refs
        y = _dot(r_ref[0], wo_ref[0:RET_WIDTH, :])
        y += _dot(m_ref[0], wo_ref[RET_WIDTH:RET_WIDTH + MLA_WIDTH, :])
        y += _dot(g_ref[0], wo_ref[RET_WIDTH + MLA_WIDTH:, :])
        x = x_ref[0] + modp_ref[0, 2:3, :] * y
        xo_ref[0] = x
        cos, sin, cos_r, sin_r = cos_ref[0], sin_ref[0], cosr_ref[0], sinr_ref[0]
    else:
        (x_ref, pos_ref, inv_ref, mod_ref, nw_ref, w_ref, qnw_ref, kvnw_ref, wq_ref, wrot_ref,
         wg_ref, bg_ref, p_ref, q_out, k_out, vt_out, la_out, cr_out, sr_out, cm_out, sm_out) = refs
        x = x_ref[0]
        cos_r, sin_r, cos, sin = _rope_rows(pos_ref[0].astype(F32), inv_ref[...])
        cr_out[0], sr_out[0], cm_out[0], sm_out[0] = cos_r, sin_r, cos, sin
    t = x.shape[0]
    shift = mod_ref[0, 0:1, :]
    scale = mod_ref[0, 1:2, :]
    hn = x * lax.rsqrt(jnp.mean(x * x, axis=-1, keepdims=True) + EPS) * nw_ref[...]
    act = (hn * (1.0 + scale) + shift).astype(BF16)

    def project(c0):
        c1 = min(c0 + IN_NCHUNK, NP_COLS)
        a = _dot(act, w_ref[:, c0:c1])
        for lo, hi, kind in _slab_segments(c0, c1):
            piece = a[:, lo - c0:hi - c0]
            if kind == "gate":
                piece = _silu(piece)
            elif kind == "rot":
                x1, x2 = piece[:, :LANE], piece[:, LANE:]
                piece = jnp.concatenate([x1 * cos_r - x2 * sin_r, x2 * cos_r + x1 * sin_r], axis=1)
            p_ref[0, :, lo:hi] = piece.astype(BF16)
        return a

    c_mq = OFF_MQ // IN_NCHUNK * IN_NCHUNK
    c_kv = OFF_KV // IN_NCHUNK * IN_NCHUNK
    assert c_kv == OFF_MB // IN_NCHUNK * IN_NCHUNK and c_mq != c_kv
    slab_q = project(c_mq)
    slab_kv = project(c_kv)
    rest = [c0 for c0 in range(0, NP_COLS, IN_NCHUNK) if c0 not in (c_mq, c_kv)]

    sm_scale = (MLA_NOPE + MLA_ROPE) ** -0.5 * LOG2E
    lat = slab_q[:, OFF_MQ - c_mq:OFF_MQ - c_mq + MLA_Q_RANK]
    lat = lat * lax.rsqrt(jnp.mean(lat * lat, axis=-1, keepdims=True) + EPS) * (qnw_ref[...] * sm_scale)
    lat = lat.astype(BF16)
    tok_chunk = (pl.program_id(1) * t + lax.broadcasted_iota(jnp.int32, (t, LANE), 0)) // CHUNK
    code_lane = lax.broadcasted_iota(jnp.int32, (t, LANE), 1) - MLA_ROPE
    q_code = jnp.where((code_lane > tok_chunk) & (code_lane < n_chunks), MASK_NEG, 0.0)
    k_code = jnp.where(code_lane == tok_chunk, 1.0, 0.0)

    kv = slab_kv[:, OFF_KV - c_kv:OFF_KV - c_kv + MLA_KV_RANK]
    kvn = kv * lax.rsqrt(jnp.mean(kv * kv, axis=-1, keepdims=True) + EPS) * kvnw_ref[...]
    k_out[0, :, :LANE] = kvn.astype(BF16)
    mb = slab_kv[:, OFF_MB - c_kv:OFF_MB - c_kv + LANE]
    mb_rot = pltpu.roll(mb, LANE - MB_ROT_LANE, 1)
    k_out[0, :, LANE:] = (mb * cos + mb_rot * sin + k_code).astype(BF16)
    gate = _dot(mb.astype(BF16), wg_ref[...]) + bg_ref[...]
    la_out[0] = (jnp.minimum(gate, 0.0) - jnp.log(1.0 + jnp.exp(-jnp.abs(gate)))) / GLA_TAU
    vt_out[0, 0:LANE, :] = kvn.T.astype(BF16)
    pad_rows = vt_out.shape[1] - LANE
    row = lax.broadcasted_iota(jnp.int32, (pad_rows, t), 0)
    vt_out[0, LANE:, :] = jnp.where(row == 0, 1.0, 0.0).astype(BF16)

    rot_all = _dot(lat, wrot_ref[...])
    per_blk = LANE // MLA_ROPE
    heads_per_slab = -(-MLA_HEADS // max(len(rest), 1))
    for h in range(MLA_HEADS):
        if h % heads_per_slab == 0 and rest:
            project(rest.pop(0))
        main = _dot(lat, wq_ref[:, 2 * LANE * h:2 * LANE * (h + 1)])
        blk = rot_all[:, LANE * (h // per_blk):LANE * (h // per_blk + 1)]
        lane0 = MLA_ROPE * (h % per_blk)
        rot = blk if lane0 == 0 else pltpu.roll(blk, LANE - lane0, 1)
        q_out[0, h, :, :LANE] = main[:, :LANE].astype(BF16)
        q_out[0, h, :, LANE:] = (main[:, LANE:] * cos + rot * sin + q_code).astype(BF16)
    for c0 in rest:
        project(c0)


def _layer_in(x, mod3, norm_w_row, w_packed, qn_row, kvn_row, wq_all, wrot, w_g2p, b_g2_row, t, *,
              positions=None, tables=None, prev=None):
    b, s, d = x.shape
    n_chunks = s // CHUNK
    assert MLA_ROPE + n_chunks <= LANE, "chunk-mask code does not fit the spare contraction lanes"
    first = prev is None
    assert first == (positions is not None) and first == (tables is None)

    def tok(w):
        return pl.BlockSpec((1, t, w), lambda i, j: (i, j, 0))

    def const(a):
        return pl.BlockSpec(a.shape, lambda i, j: (0,) * a.ndim)

    mod_spec = pl.BlockSpec((1, 3, d), lambda i, j: (i, 0, 0))
    common_specs = [mod_spec, const(norm_w_row), const(w_packed)]
    common_args = [mod3, norm_w_row, w_packed]
    tail_specs = [const(qn_row), const(kvn_row), const(wq_all), const(wrot), const(w_g2p), const(b_g2_row)]
    tail_args = [qn_row, kvn_row, wq_all, wrot, w_g2p, b_g2_row]
    out_specs = [tok(NP_COLS),
                 pl.BlockSpec((1, MLA_HEADS, t, 2 * LANE), lambda i, j: (i, 0, j, 0)),
                 tok(2 * LANE),
                 pl.BlockSpec((1, VT_ROWS, t), lambda i, j: (i, 0, j)),
                 tok(GLA_KWIDTH)]
    out_shape = [jax.ShapeDtypeStruct((b, s, NP_COLS), BF16),
                 jax.ShapeDtypeStruct((b, MLA_HEADS, s, 2 * LANE), BF16),
                 jax.ShapeDtypeStruct((b, s, 2 * LANE), BF16),
                 jax.ShapeDtypeStruct((b, VT_ROWS, s), BF16),
                 jax.ShapeDtypeStruct((b, s, GLA_KWIDTH), F32)]
    if first:
        inv_row = _rope_inv_row()
        in_specs = [tok(d), tok(1), const(inv_row)] + common_specs + tail_specs
        args = [x, positions.reshape(b, s, 1), inv_row] + common_args + tail_args
        out_specs = out_specs + [tok(LANE)] * 4
        out_shape = out_shape + [jax.ShapeDtypeStruct((b, s, LANE), F32)] * 4
    else:
        r_o, m_o, g_o, mod3_prev, w_out_bf = prev
        in_specs = ([tok(RET_WIDTH), tok(MLA_WIDTH), tok(GLA_WIDTH), tok(d), mod_spec, const(w_out_bf)]
                    + common_specs + [tok(LANE)] * len(tables) + tail_specs)
        args = [r_o, m_o, g_o, x, mod3_prev, w_out_bf] + common_args + list(tables) + tail_args
        out_specs = [tok(d)] + out_specs
        out_shape = [jax.ShapeDtypeStruct((b, s, d), F32)] + out_shape
    outs = pl.pallas_call(
        functools.partial(_layer_in_kernel, prev=not first, n_chunks=n_chunks),
        grid=(b, s // t),
        in_specs=in_specs,
        out_specs=out_specs,
        out_shape=out_shape,
        compiler_params=_cparams(("arbitrary", "arbitrary")),
        name="layer_in",
    )(*args)
    return [x] + list(outs) if first else list(outs)


def _out_proj_kernel(r_ref, m_ref, g_ref, x_ref, mod_ref, w_ref, fw_ref, o_ref):
    y = _dot(r_ref[0], w_ref[0:RET_WIDTH, :])
    y += _dot(m_ref[0], w_ref[RET_WIDTH:RET_WIDTH + MLA_WIDTH, :])
    y += _dot(g_ref[0], w_ref[RET_WIDTH + MLA_WIDTH:, :])
    gate = mod_ref[0, 2:3, :]
    x = x_ref[0] + gate * y
    o_ref[0] = x * lax.rsqrt(jnp.mean(x * x, axis=-1, keepdims=True) + EPS) * fw_ref[...]


def _out_proj(r_o, m_o, g_o, x, mod3, w_out_bf, final_row, t):
    b, s, d = x.shape

    def tok(w):
        return pl.BlockSpec((1, t, w), lambda i, j: (i, j, 0))

    return pl.pallas_call(
        _out_proj_kernel,
        grid=(b, s // t),
        in_specs=[tok(RET_WIDTH), tok(MLA_WIDTH), tok(GLA_WIDTH), tok(d),
                  pl.BlockSpec((1, 3, d), lambda i, j: (i, 0, 0)),
                  pl.BlockSpec(w_out_bf.shape, lambda i, j: (0, 0)),
                  pl.BlockSpec((1, d), lambda i, j: (0, 0))],
        out_specs=tok(d),
        out_shape=jax.ShapeDtypeStruct((b, s, d), F32),
        compiler_params=_cparams(("arbitrary", "arbitrary")),
        name="out_proj",
    )(r_o, m_o, g_o, x, mod3, w_out_bf, final_row)


def _pack_w_in(w_in_l):
    d = w_in_l.shape[0]
    cuts = np.cumsum([RET_WIDTH] * 4 + [MLA_Q_RANK, MLA_KV_RANK, MLA_ROPE, MLA_WIDTH]
                     + [GLA_KWIDTH, GLA_KWIDTH, GLA_WIDTH, GLA_GATE_RANK, GLA_WIDTH])[:-1]
    rq, rk, rv, rz, mq, kv, kr, mz, gq, gk, gv, gg, gz = jnp.split(w_in_l, [int(c) for c in cuts], axis=1)

    def halves_first(w):
        w4 = w.reshape(d, RET_HEADS, 2, RET_HEAD_DIM // 2)
        return jnp.swapaxes(w4, 1, 2).reshape(d, RET_WIDTH)

    hm = MLA_ROPE // 2
    kr_rot = jnp.concatenate([-kr[:, hm:], kr[:, :hm]], axis=1)
    mb_pad = jnp.zeros((d, LANE - MB_ROT_LANE - MLA_ROPE), F32)
    assert MB_ROT_LANE == MLA_ROPE + GLA_GATE_RANK
    packed = {
        OFF_MZ: mz, OFF_RQ: halves_first(rq), OFF_RK: halves_first(rk), OFF_RV: rv, OFF_RZ: rz,
        OFF_MQ: mq, OFF_GV: gv, OFF_GZ: gz, OFF_KV: kv,
        OFF_MB: jnp.concatenate([kr, gg, kr_rot, mb_pad], axis=1), OFF_GQ: gq, OFF_GK: gk,
    }
    pieces, pos = [], 0
    for off in sorted(packed):
        assert off == pos, "packed in-projection layout has a gap or overlap"
        pieces.append(packed[off])
        pos += packed[off].shape[1]
    assert pos == NP_COLS
    return jnp.concatenate(pieces, axis=1).astype(BF16)


def _pack_mla_q(w_uq_l, w_abs_l):
    r = w_uq_l.shape[0]
    hd = MLA_NOPE + MLA_ROPE
    hm = MLA_ROPE // 2
    w3 = w_uq_l.reshape(r, MLA_HEADS, hd)
    pe = w3[:, :, MLA_NOPE:]
    pad = jnp.zeros((r, MLA_HEADS, LANE - MLA_ROPE), F32)
    main = jnp.concatenate([jnp.moveaxis(w_abs_l, 0, 1), pe, pad], axis=-1)
    rot = jnp.concatenate([-pe[:, :, hm:], pe[:, :, :hm]], axis=-1)
    return (main.reshape(r, MLA_HEADS * 2 * LANE).astype(BF16),
            rot.reshape(r, MLA_HEADS * MLA_ROPE).astype(BF16))


def kernel(x, c, positions, norm_w, ada_w, ada_b, w_in, mla_q_norm, w_uq, mla_kv_norm, w_ukv,
           gla_w_g2, gla_b_g2, gla_norm, w_out, final_norm):
    b, s, d = x.shape
    depth = w_in.shape[0]
    t_tok = min(512, s)

    mod = _ada_mod(c, ada_w, ada_b).reshape(depth, b, 3, d)

    kv_hd = MLA_NOPE + MLA_V
    q_hd = MLA_NOPE + MLA_ROPE
    w_ukv4 = w_ukv.reshape(depth, MLA_KV_RANK, MLA_HEADS, kv_hd)
    wk_nope = jnp.moveaxis(w_ukv4[..., :MLA_NOPE], 2, 1)
    wuv = jnp.transpose(w_ukv4[..., MLA_NOPE:], (0, 2, 3, 1)).astype(BF16)
    wq_nope = jnp.moveaxis(
        w_uq.reshape(depth, MLA_Q_RANK, MLA_HEADS, q_hd)[..., :MLA_NOPE], 2, 1)
    w_abs = _absorb_weights(wq_nope, wk_nope)

    prev = None
    for l in range(depth):
        w_packed = _pack_w_in(w_in[l])
        wq_all, wrot = _pack_mla_q(w_uq[l], w_abs[l])
        w_g2p = jnp.zeros((LANE, GLA_KWIDTH), F32).at[MLA_ROPE:MLA_ROPE + GLA_GATE_RANK].set(
            gla_w_g2[l]).astype(BF16)

        front = (x, mod[l], norm_w[l].reshape(1, d), w_packed, mla_q_norm[l].reshape(1, MLA_Q_RANK),
                 mla_kv_norm[l].reshape(1, MLA_KV_RANK), wq_all, wrot, w_g2p,
                 gla_b_g2[l].reshape(1, GLA_KWIDTH), t_tok)
        if prev is None:
            x, proj, q_s, k_c, v_t, log_a, cos_r, sin_r, cos_m, sin_m = _layer_in(
                *front, positions=positions)
        else:
            x, proj, q_s, k_c, v_t, log_a = _layer_in(
                *front, tables=(cos_m, sin_m, cos_r, sin_r), prev=prev)
        r_o, g_o = _linear_mixers(proj, log_a, jnp.tile(gla_norm[l], GLA_HEADS).reshape(1, GLA_WIDTH))
        m_o = _mla_attn(q_s, k_c, v_t, proj, wuv[l])
        prev = (r_o, m_o, g_o, mod[l], w_out[l].astype(BF16))
    r_o, m_o, g_o, mod_last, w_out_bf = prev
    t_out = OUT_TILE if s % OUT_TILE == 0 else t_tok
    return _out_proj(r_o, m_o, g_o, x, mod_last, w_out_bf, final_norm.reshape(1, d), t_out)
```

```python
import functools

import numpy as np

import jax
import jax.numpy as jnp
from jax import lax
from jax.experimental import pallas as pl
from jax.experimental.pallas import tpu as pltpu

F32 = jnp.float32
BF16 = jnp.bfloat16

D_MODEL = 1024
CHUNK = 64
EPS = 1e-6
ROPE_THETA = 10000.0

RET_HEADS = 4
RET_HEAD_DIM = 64
RET_WIDTH = 256
MLA_HEADS = 8
MLA_NOPE = 64
MLA_ROPE = 32
MLA_V = 64
MLA_WIDTH = 512
MLA_Q_RANK = 256
MLA_KV_RANK = 128
GLA_HEADS = 4
GLA_DK = 32
GLA_DV = 64
GLA_KWIDTH = 128
GLA_WIDTH = 256
GLA_GATE_RANK = 16
GLA_TAU = 16.0
IN_COLS = 2736

LANE = 128

OFF_MZ = 0
OFF_RQ = 512
OFF_RK = 768
OFF_RV = 1024
OFF_RZ = 1280
OFF_MQ = 1536
OFF_GV = 1792
OFF_GZ = 2048
OFF_KV = 2304
OFF_MB = 2432
MB_ROT_LANE = 48
OFF_GQ = 2560
OFF_GK = 2688
NP_COLS = 2816

MIX_TILE = 256
MIX_SEQS = 4
ATT_T = 256
ATT_SEQS = 4
VT_ROWS = 144
LOG2E = 1.4426950408889634
MASK_NEG = -1e30
OUT_TILE = 1024
VMEM_LIMIT = 56 * 1024 * 1024


def _cparams(sem, flags=None):
    return pltpu.CompilerParams(dimension_semantics=sem, vmem_limit_bytes=VMEM_LIMIT, flags=flags)


def _dot(a, b):
    return jnp.dot(a, b, preferred_element_type=F32)


def _dot_nt(a, b):
    return lax.dot_general(a, b, (((1,), (1,)), ((), ())), preferred_element_type=F32)


def _dot_tn(a, b):
    return lax.dot_general(a, b, (((0,), (0,)), ((), ())), preferred_element_type=F32)


def _silu(x):
    return x / (1.0 + jnp.exp(-x))


def _ada_kernel(c_ref, w_ref, b_ref, o_ref):
    c = c_ref[...]
    o_ref[0] = _dot(_silu(c), w_ref[0]) + b_ref[0]


def _ada_mod(c, ada_w, ada_b):
    depth, d, d3 = ada_w.shape
    b = c.shape[0]
    nblk = d3 // d
    return pl.pallas_call(
        _ada_kernel,
        grid=(depth, nblk),
        in_specs=[
            pl.BlockSpec((b, d), lambda l, j: (0, 0)),
            pl.BlockSpec((1, d, d), lambda l, j: (l, 0, j)),
            pl.BlockSpec((1, 1, d), lambda l, j: (l, 0, j)),
        ],
        out_specs=pl.BlockSpec((1, b, d), lambda l, j: (l, 0, j)),
        out_shape=jax.ShapeDtypeStruct((depth, b, d3), F32),
        compiler_params=_cparams(("arbitrary", "arbitrary")),
        name="ada_mod",
    )(c, ada_w, ada_b.reshape(depth, 1, d3))


def _rope_rows(pos, inv_row):
    half_r = RET_HEAD_DIM // 2
    half_m = MLA_ROPE // 2
    t2 = pos.shape[0] // 2
    lane = lax.broadcasted_iota(jnp.int32, (t2, LANE), 1)
    ang = jnp.where(lane < LANE // 2, pos[:t2], pos[t2:]) * inv_row
    is_r = lane < half_r
    is_m = (lane >= half_r) & (lane < half_r + half_m)

    def tables(x):
        r = jnp.where(is_r, x, 0.0)
        r = r + pltpu.roll(r, half_r, 1)
        r = r + pltpu.roll(r, 2 * half_r, 1)
        m = pltpu.roll(jnp.where(is_m, x, 0.0), LANE - half_r, 1)
        m = m + pltpu.roll(m, half_m, 1)
        return r, m

    def both_halves(x):
        r_lo, m_lo = tables(x)
        r_hi, m_hi = tables(pltpu.roll(x, LANE // 2, 1))
        return jnp.concatenate([r_lo, r_hi], axis=0), jnp.concatenate([m_lo, m_hi], axis=0)

    cos_r, cos_m = both_halves(jnp.cos(ang))
    sin_r, sin_m = both_halves(jnp.sin(ang))
    return cos_r, sin_r, cos_m, sin_m


def _rope_inv_row():
    half_r = RET_HEAD_DIM // 2
    half_m = MLA_ROPE // 2
    inv_r = ROPE_THETA ** (-jnp.arange(half_r, dtype=F32) / half_r)
    inv_m = ROPE_THETA ** (-jnp.arange(half_m, dtype=F32) / half_m)
    one = jnp.concatenate([inv_r, inv_m, jnp.zeros((LANE // 2 - half_r - half_m,), F32)])
    return jnp.tile(one, 2).reshape(1, LANE)


IN_NCHUNK = 1024


def _retention_phases(q_ref, k_ref, v_ref, z_ref, dmat_ref, qw_ref, kw_ref,
                      sdec_ref, bd_ref, ones_ref, o_ref, state_ref):
    nb = q_ref.shape[0]

    lane = lax.broadcasted_iota(jnp.int32, (1, RET_WIDTH), 1)
    qk_head = (lane % LANE) // (RET_HEAD_DIM // 2)
    v_head = lane // RET_HEAD_DIM
    zero_bf = jnp.zeros((), BF16)

    def stack(a, head_of_lane):
        return jnp.concatenate(
            [jnp.where(head_of_lane == h, a, zero_bf) for h in range(RET_HEADS)], axis=0)

    seqs = range(nb)
    q = [q_ref[i] for i in seqs]
    k = [k_ref[i] for i in seqs]
    v = [v_ref[i] for i in seqs]
    state = [state_ref[i] for i in seqs]
    u = [_dot_tn((k[i].astype(F32) * kw_ref[...]).astype(BF16), v[i]) for i in seqs]
    inter = [_dot((q[i].astype(F32) * qw_ref[...]).astype(BF16), state[i].astype(BF16)) for i in seqs]
    yield
    scores = [(_dot_nt(q[i], stack(k[i], qk_head)) * dmat_ref[...]).astype(BF16) for i in seqs]
    for i in seqs:
        state_ref[i] = state[i] * sdec_ref[...] + u[i] * bd_ref[...]
    yield
    intra = [_dot(scores[i], stack(v[i], v_head)) for i in seqs]
    yield
    o = [intra[i] + inter[i] for i in seqs]
    ms = [_dot((o[i] * o[i]).astype(BF16), ones_ref[...]) for i in seqs]
    yield
    for i in seqs:
        y = o[i] * lax.rsqrt(ms[i] + EPS)
        o_ref[i] = (y * z_ref[i].astype(F32)).astype(BF16)


def _retention_consts(t):
    f32 = F32
    h = RET_HEADS
    log_gamma = jnp.log1p(-jnp.exp2(-5.0 - jnp.arange(h, dtype=f32)))
    idx = jnp.arange(t, dtype=f32)
    dist = jnp.abs(idx[:, None] - idx[None, :])
    ci = jnp.arange(t) // CHUNK
    vis = (ci[None, :] <= ci[:, None]).astype(f32)
    k_scale = RET_HEAD_DIM ** -0.5
    dmat = jnp.exp(log_gamma[:, None, None] * dist[None]) * vis[None] * k_scale
    dmat = jnp.moveaxis(dmat, 0, 1).reshape(t, h * t)
    lane = jnp.arange(RET_WIDTH)
    qk_head = (lane % LANE) // (RET_HEAD_DIM // 2)
    v_head = lane // RET_HEAD_DIM
    lg_lane = log_gamma[qk_head]
    qw = jnp.exp((idx + 1.0)[:, None] * lg_lane[None, :])
    kw = jnp.exp((t - 1.0 - idx)[:, None] * lg_lane[None, :]) * k_scale
    bd = (qk_head[:, None] == v_head[None, :]).astype(f32)
    sdec = jnp.exp(t * lg_lane)[:, None] * bd
    ones = ((v_head[:, None] == v_head[None, :]).astype(f32) / RET_HEAD_DIM).astype(BF16)
    return dmat, qw, kw, sdec, bd, ones


def _gla_phases(q_ref, k_ref, v_ref, z_ref, la_ref, ltri_ref, bdt_ref, ones_ref,
                nw_ref, o_ref, state_ref):
    nb, t = q_ref.shape[0], q_ref.shape[1]
    nchunk = t // CHUNK

    lane_k = lax.broadcasted_iota(jnp.int32, (1, GLA_KWIDTH), 1) // GLA_DK
    lane_v = lax.broadcasted_iota(jnp.int32, (1, GLA_WIDTH), 1) // GLA_DV
    row_i = lax.broadcasted_iota(jnp.int32, (CHUNK, GLA_HEADS * CHUNK), 0)
    col_j = lax.broadcasted_iota(jnp.int32, (CHUNK, GLA_HEADS * CHUNK), 1) % CHUNK
    causal = row_i >= col_j
    zero_bf = jnp.zeros((), BF16)
    ltri = ltri_ref[...]
    bdt = bdt_ref[...]
    ones = ones_ref[...]
    nw = nw_ref[...]

    def stack(a, head_of_lane):
        return jnp.concatenate(
            [jnp.where(head_of_lane == h, a, zero_bf) for h in range(GLA_HEADS)], axis=0)

    seqs = range(nb)
    log_a = [la_ref[i] for i in seqs]
    la_hi = [x.astype(BF16) for x in log_a]
    la_lo = [(x - h.astype(F32)).astype(BF16) for x, h in zip(log_a, la_hi)]
    cum = [_dot(ltri, h) + _dot(ltri, l) for h, l in zip(la_hi, la_lo)]
    yield
    q = [q_ref[i].astype(F32) for i in seqs]
    k = [k_ref[i].astype(F32) * (GLA_DK ** -0.5) for i in seqs]
    e_pos = [jnp.exp(x) for x in cum]
    e_neg = [jnp.exp(-x) for x in cum]
    q_pos = [(a * e).astype(BF16) for a, e in zip(q, e_pos)]
    q_neg = [(a * e).astype(BF16) for a, e in zip(q, e_neg)]
    k_neg = [(a * e).astype(BF16) for a, e in zip(k, e_neg)]
    k_pos = [(a * e).astype(BF16) for a, e in zip(k, e_pos)]

    units = [(i, slice(c * CHUNK, (c + 1) * CHUNK)) for i in seqs for c in range(nchunk)]
    v_c = [v_ref[i, r, :] for i, r in units]
    last = [cum[i][r.stop - 1:r.stop, :] for i, r in units]
    k_st = [(k[i][r] * jnp.exp(l - cum[i][r])).astype(BF16) for (i, r), l in zip(units, last)]
    u_t = [_dot_tn(vc, ks) for vc, ks in zip(v_c, k_st)]
    yield
    past = [_dot_nt(q_pos[i][r], stack(k_neg[i][r], lane_k)) for i, r in units]
    fut = [_dot_nt(q_neg[i][r], stack(k_pos[i][r], lane_k)) for i, r in units]
    attn = [jnp.where(causal, p, f).astype(BF16) for p, f in zip(past, fut)]
    yield
    intra = [_dot(a, stack(vc, lane_v)) for a, vc in zip(attn, v_c)]
    yield
    states = []
    for i in seqs:
        st = state_ref[i]
        for c in range(nchunk):
            states.append(st)
            st = st * jnp.exp(last[i * nchunk + c]) + u_t[i * nchunk + c] * bdt
        state_ref[i] = st
    inter = [_dot_nt(q_pos[i][r], st.astype(BF16)) for (i, r), st in zip(units, states)]
    yield
    o = [a + b for a, b in zip(intra, inter)]
    ms = [_dot((x * x).astype(BF16), ones) for x in o]
    yield
    for (i, r), x, m in zip(units, o, ms):
        y = x * lax.rsqrt(m + EPS) * nw
        o_ref[i, r, :] = (y * z_ref[i, r, :].astype(F32)).astype(BF16)


N_RET_IN = 10
N_RET_SEQ = 4
N_GLA_IN = 9
N_GLA_SEQ = 5
GROUP_ORDER = "GRGRGGRGGRGR"
MIX_GROUPS = 4
MIX_DELAY = 7


def _mixer_order(n_groups):
    order, step = [], 0
    while True:
        active = [(g, step - g * MIX_DELAY) for g in range(n_groups)
                  if 0 <= step - g * MIX_DELAY < len(GROUP_ORDER)]
        if not active and step >= (n_groups - 1) * MIX_DELAY + len(GROUP_ORDER):
            return order
        order += [f"{GROUP_ORDER[k]}{g}" for g, k in active]
        step += 1


def _linear_mixers_kernel(*refs):
    ret_in = refs[:N_RET_IN]
    gla_in = refs[N_RET_IN:N_RET_IN + N_GLA_IN]
    ret_out, gla_out, ret_state, gla_state = refs[N_RET_IN + N_GLA_IN:]
    nb = ret_out.shape[0]

    @pl.when(pl.program_id(1) == 0)
    def _():
        ret_state[...] = jnp.zeros_like(ret_state)
        gla_state[...] = jnp.zeros_like(gla_state)

    n_groups = MIX_GROUPS if nb % MIX_GROUPS == 0 else 1
    per = nb // n_groups
    phases = {}
    for grp in range(n_groups):
        def sub(r, grp=grp):
            return r.at[pl.ds(grp * per, per)]
        phases[f"R{grp}"] = _retention_phases(
            *[sub(r) for r in ret_in[:N_RET_SEQ]], *ret_in[N_RET_SEQ:], sub(ret_out), sub(ret_state))
        phases[f"G{grp}"] = _gla_phases(
            *[sub(r) for r in gla_in[:N_GLA_SEQ]], *gla_in[N_GLA_SEQ:], sub(gla_out), sub(gla_state))
    for who in _mixer_order(n_groups):
        next(phases[who], None)
    for gen in phases.values():
        assert next(gen, "done") == "done", "the phase order does not cover every phase"


def _linear_mixers(proj, log_a, gla_norm_row):
    b, s, _ = proj.shape
    t = min(MIX_TILE, s)
    nb = MIX_SEQS if b % MIX_SEQS == 0 else 1
    dmat, qw, kw, sdec, bd, r_ones = _retention_consts(t)
    idx = jnp.arange(t)
    ltri = ((idx[:, None] >= idx[None, :]) & (idx[:, None] // CHUNK == idx[None, :] // CHUNK)).astype(BF16)
    kh = jnp.arange(GLA_KWIDTH) // GLA_DK
    vh = jnp.arange(GLA_WIDTH) // GLA_DV
    bdt = (vh[:, None] == kh[None, :]).astype(F32)
    g_ones = ((vh[:, None] == vh[None, :]).astype(F32) / GLA_DV).astype(BF16)

    def col(off, w):
        return pl.BlockSpec((nb, t, w), lambda i, j, o=off // w: (i, j, o))

    def const(a):
        return pl.BlockSpec(a.shape, lambda i, j: (0, 0))

    ret_consts = [dmat, qw, kw, sdec, bd, r_ones]
    gla_consts = [ltri, bdt, g_ones, gla_norm_row]
    in_specs = ([col(OFF_RQ, RET_WIDTH), col(OFF_RK, RET_WIDTH), col(OFF_RV, RET_WIDTH),
                 col(OFF_RZ, RET_WIDTH)] + [const(a) for a in ret_consts]
                + [col(OFF_GQ, GLA_KWIDTH), col(OFF_GK, GLA_KWIDTH), col(OFF_GV, GLA_WIDTH),
                   col(OFF_GZ, GLA_WIDTH), pl.BlockSpec((nb, t, GLA_KWIDTH), lambda i, j: (i, j, 0))]
                + [const(a) for a in gla_consts])
    args = [proj] * 4 + ret_consts + [proj] * 4 + [log_a] + gla_consts
    assert len(in_specs) == N_RET_IN + N_GLA_IN == len(args)
    return pl.pallas_call(
        _linear_mixers_kernel,
        grid=(b // nb, s // t),
        in_specs=in_specs,
        out_specs=[pl.BlockSpec((nb, t, RET_WIDTH), lambda i, j: (i, j, 0)),
                   pl.BlockSpec((nb, t, GLA_WIDTH), lambda i, j: (i, j, 0))],
        out_shape=[jax.ShapeDtypeStruct((b, s, RET_WIDTH), BF16),
                   jax.ShapeDtypeStruct((b, s, GLA_WIDTH), BF16)],
        scratch_shapes=[pltpu.VMEM((nb, RET_WIDTH, RET_WIDTH), F32),
                        pltpu.VMEM((nb, GLA_WIDTH, GLA_KWIDTH), F32)],
        compiler_params=_cparams(("arbitrary", "arbitrary")),
        name="linear_mixers",
    )(*args)


def _absorb_kernel(wq_ref, wk_ref, o_ref):
    o_ref[0, 0] = lax.dot_general(wq_ref[0, 0], wk_ref[0, 0], (((1,), (1,)), ((), ())),
                                  precision=lax.Precision.HIGHEST, preferred_element_type=F32)


def _absorb_weights(wq_nope, wk_nope):
    depth, h, r, dn = wq_nope.shape
    kr = wk_nope.shape[2]
    return pl.pallas_call(
        _absorb_kernel,
        grid=(depth, h),
        in_specs=[pl.BlockSpec((1, 1, r, dn), lambda l, i: (l, i, 0, 0)),
                  pl.BlockSpec((1, 1, kr, dn), lambda l, i: (l, i, 0, 0))],
        out_specs=pl.BlockSpec((1, 1, r, kr), lambda l, i: (l, i, 0, 0)),
        out_shape=jax.ShapeDtypeStruct((depth, h, r, kr), F32),
        compiler_params=_cparams(("arbitrary", "arbitrary")),
        name="mla_absorb",
    )(wq_nope, wk_nope)


def _mla_attn_kernel(q_ref, k_ref, vt_ref, z_ref, wuv_ref, o_ref, m_ref, acc_ref,
                     s0_ref, s1_ref, mt0_ref, mt1_ref):
    nb, tq = q_ref.shape[0], q_ref.shape[2]
    t = ATT_T
    n_tiles = pl.program_id(1) + 1
    units = [(i, h) for i in range(nb) for h in range(MLA_HEADS)]

    def key_start(kt):
        return pl.multiple_of(jnp.minimum(kt, n_tiles - 1) * t, t)

    def scores(i, h, k0, s_ref, mt_ref):
        cols = slice(h * tq, (h + 1) * tq)
        s = _dot_nt(k_ref[i, pl.ds(k0, t), :], q_ref[i, h])
        s_ref[i, :, cols] = s
        mt_ref[i, :, cols] = jnp.max(s, axis=0, keepdims=True)

    def accumulate(i, h, k0, s_ref, mt_ref, first=False):
        cols = slice(h * tq, (h + 1) * tq)
        m_new = mt_ref[i, :, cols]
        if not first:
            m_prev = m_ref[i, :, cols]
            m_new = jnp.maximum(m_prev, m_new)
        p = jnp.exp2(s_ref[i, :, cols] - m_new).astype(BF16)
        pv = _dot(vt_ref[i, :, pl.ds(k0, t)], p)
        if not first:
            pv = jnp.exp2(m_prev - m_new) * acc_ref[i, :, cols] + pv
        acc_ref[i, :, cols] = pv
        m_ref[i, :, cols] = m_new

    def overlapped(k_next, k_cur, s_next, mt_next, s_cur, mt_cur, first=False):
        for i, h in units:
            accumulate(i, h, k_cur, s_cur, mt_cur, first)
            scores(i, h, k_next, s_next, mt_next)

    for i, h in units:
        scores(i, h, 0, s0_ref, mt0_ref)
    overlapped(key_start(1), 0, s1_ref, mt1_ref, s0_ref, mt0_ref, first=True)

    def pair(j, carry):
        k_a, k_b, k_c = key_start(2 * j + 1), key_start(2 * j + 2), key_start(2 * j + 3)
        overlapped(k_b, k_a, s0_ref, mt0_ref, s1_ref, mt1_ref)
        overlapped(k_c, k_b, s1_ref, mt1_ref, s0_ref, mt0_ref)
        return carry

    lax.fori_loop(0, (n_tiles - 1) // 2, pair, 0)

    @pl.when((n_tiles - 1) % 2 == 1)
    def _():
        for i, h in units:
            accumulate(i, h, key_start(n_tiles - 1), s1_ref, mt1_ref)

    o_h = [(acc_ref[i, 0:LANE, h * tq:(h + 1) * tq]
            * (1.0 / acc_ref[i, LANE:LANE + 1, h * tq:(h + 1) * tq])).astype(BF16) for i, h in units]
    o_t = [_dot(wuv_ref[h], o) for (i, h), o in zip(units, o_h)]
    for i in range(nb):
        o_seq = jnp.concatenate([o for (ii, h), o in zip(units, o_t) if ii == i], axis=0)
        o_ref[i] = (o_seq.T * z_ref[i].astype(F32)).astype(BF16)


def _mla_attn(q_s, k_c, v_t, proj, wuv):
    b, h, s, _ = q_s.shape
    t = ATT_T
    nq = h * t
    vrows = v_t.shape[1]
    nb = ATT_SEQS if b % ATT_SEQS == 0 else 1
    return pl.pallas_call(
        _mla_attn_kernel,
        grid=(b // nb, s // t),
        in_specs=[pl.BlockSpec((nb, h, t, 2 * LANE), lambda i, j: (i, 0, j, 0)),
                  pl.BlockSpec((nb, s, 2 * LANE), lambda i, j: (i, 0, 0)),
                  pl.BlockSpec((nb, vrows, s), lambda i, j: (i, 0, 0)),
                  pl.BlockSpec((nb, t, MLA_WIDTH), lambda i, j: (i, j, OFF_MZ // MLA_WIDTH)),
                  pl.BlockSpec(wuv.shape, lambda i, j: (0, 0, 0))],
        out_specs=pl.BlockSpec((nb, t, MLA_WIDTH), lambda i, j: (i, j, 0)),
        out_shape=jax.ShapeDtypeStruct((b, s, MLA_WIDTH), BF16),
        scratch_shapes=[pltpu.VMEM((nb, 1, nq), F32), pltpu.VMEM((nb, vrows, nq), F32),
                        pltpu.VMEM((nb, t, nq), F32), pltpu.VMEM((nb, t, nq), F32),
                        pltpu.VMEM((nb, 1, nq), F32), pltpu.VMEM((nb, 1, nq), F32)],
        compiler_params=_cparams(("arbitrary", "arbitrary")),
        name="mla_attn",
    )(q_s, k_c, v_t, proj, wuv)


def _slab_segments(c0, c1):
    special = sorted([(OFF_RQ, OFF_RQ + RET_WIDTH, "rot"), (OFF_RK, OFF_RK + RET_WIDTH, "rot"),
                      (OFF_MZ, OFF_MZ + MLA_WIDTH, "gate"), (OFF_RZ, OFF_RZ + RET_WIDTH, "gate"),
                      (OFF_GZ, OFF_GZ + GLA_WIDTH, "gate")])
    out, pos = [], c0
    for lo, hi, kind in special:
        if hi <= c0 or c1 <= lo:
            continue
        assert c0 <= lo and hi <= c1, "a special column block must lie inside one slab"
        if pos < lo:
            out.append((pos, lo, "plain"))
        out.append((lo, hi, kind))
        pos = hi
    if pos < c1:
        out.append((pos, c1, "plain"))
    return out


def _layer_in_kernel(*refs, prev, n_chunks):
    if prev:
        (r_ref, m_ref, g_ref, x_ref, modp_ref, wo_ref, mod_ref, nw_ref, w_ref, cos_ref, sin_ref,
         cosr_ref, sinr_ref, qnw_ref, kvnw_ref, wq_ref, wrot_ref, wg_ref, bg_ref,
         xo_ref, p_ref, q_out, k_out, vt_out, la_out) = refs
        y = _dot(r_ref[0], wo_ref[0:RET_WIDTH, :])
        y += _dot(m_ref[0], wo_ref[RET_WIDTH:RET_WIDTH + MLA_WIDTH, :])
        y += _dot(g_ref[0], wo_ref[RET_WIDTH + MLA_WIDTH:, :])
        x = x_ref[0] + modp_ref[0, 2:3, :] * y
        xo_ref[0] = x
        cos, sin, cos_r, sin_r = cos_ref[0], sin_ref[0], cosr_ref[0], sinr_ref[0]
    else:
        (x_ref, pos_ref, inv_ref, mod_ref, nw_ref, w_ref, qnw_ref, kvnw_ref, wq_ref, wrot_ref,
         wg_ref, bg_ref, p_ref, q_out, k_out, vt_out, la_out, cr_out, sr_out, cm_out, sm_out) = refs
        x = x_ref[0]
        cos_r, sin_r, cos, sin = _rope_rows(pos_ref[0].astype(F32), inv_ref[...])
        cr_out[0], sr_out[0], cm_out[0], sm_out[0] = cos_r, sin_r, cos, sin
    t = x.shape[0]
    shift = mod_ref[0, 0:1, :]
    scale = mod_ref[0, 1:2, :]
    hn = x * lax.rsqrt(jnp.mean(x * x, axis=-1, keepdims=True) + EPS) * nw_ref[...]
    act = (hn * (1.0 + scale) + shift).astype(BF16)

    def project(c0):
        c1 = min(c0 + IN_NCHUNK, NP_COLS)
        a = _dot(act, w_ref[:, c0:c1])
        for lo, hi, kind in _slab_segments(c0, c1):
            piece = a[:, lo - c0:hi - c0]
            if kind == "gate":
                piece = _silu(piece)
            elif kind == "rot":
                x1, x2 = piece[:, :LANE], piece[:, LANE:]
                piece = jnp.concatenate([x1 * cos_r - x2 * sin_r, x2 * cos_r + x1 * sin_r], axis=1)
            p_ref[0, :, lo:hi] = piece.astype(BF16)
        return a

    c_mq = OFF_MQ // IN_NCHUNK * IN_NCHUNK
    c_kv = OFF_KV // IN_NCHUNK * IN_NCHUNK
    assert c_kv == OFF_MB // IN_NCHUNK * IN_NCHUNK and c_mq != c_kv
    slab_q = project(c_mq)
    slab_kv = project(c_kv)
    rest = [c0 for c0 in range(0, NP_COLS, IN_NCHUNK) if c0 not in (c_mq, c_kv)]

    sm_scale = (MLA_NOPE + MLA_ROPE) ** -0.5 * LOG2E
    lat = slab_q[:, OFF_MQ - c_mq:OFF_MQ - c_mq + MLA_Q_RANK]
    lat = lat * lax.rsqrt(jnp.mean(lat * lat, axis=-1, keepdims=True) + EPS) * (qnw_ref[...] * sm_scale)
    lat = lat.astype(BF16)
    tok_chunk = (pl.program_id(1) * t + lax.broadcasted_iota(jnp.int32, (t, LANE), 0)) // CHUNK
    code_lane = lax.broadcasted_iota(jnp.int32, (t, LANE), 1) - MLA_ROPE
    q_code = jnp.where((code_lane > tok_chunk) & (code_lane < n_chunks), MASK_NEG, 0.0)
    k_code = jnp.where(code_lane == tok_chunk, 1.0, 0.0)

    kv = slab_kv[:, OFF_KV - c_kv:OFF_KV - c_kv + MLA_KV_RANK]
    kvn = kv * lax.rsqrt(jnp.mean(kv * kv, axis=-1, keepdims=True) + EPS) * kvnw_ref[...]
    k_out[0, :, :LANE] = kvn.astype(BF16)
    mb = slab_kv[:, OFF_MB - c_kv:OFF_MB - c_kv + LANE]
    mb_rot = pltpu.roll(mb, LANE - MB_ROT_LANE, 1)
    k_out[0, :, LANE:] = (mb * cos + mb_rot * sin + k_code).astype(BF16)
    gate = _dot(mb.astype(BF16), wg_ref[...]) + bg_ref[...]
    la_out[0] = (jnp.minimum(gate, 0.0) - jnp.log(1.0 + jnp.exp(-jnp.abs(gate)))) / GLA_TAU
    vt_out[0, 0:LANE, :] = kvn.T.astype(BF16)
    pad_rows = vt_out.shape[1] - LANE
    row = lax.broadcasted_iota(jnp.int32, (pad_rows, t), 0)
    vt_out[0, LANE:, :] = jnp.where(row == 0, 1.0, 0.0).astype(BF16)

    rot_all = _dot(lat, wrot_ref[...])
    per_blk = LANE // MLA_ROPE
    heads_per_slab = -(-MLA_HEADS // max(len(rest), 1))
    for h in range(MLA_HEADS):
        if h % heads_per_slab == 0 and rest:
            project(rest.pop(0))
        main = _dot(lat, wq_ref[:, 2 * LANE * h:2 * LANE * (h + 1)])
        blk = rot_all[:, LANE * (h // per_blk):LANE * (h // per_blk + 1)]
        lane0 = MLA_ROPE * (h % per_blk)
        rot = blk if lane0 == 0 else pltpu.roll(blk, LANE - lane0, 1)
        q_out[0, h, :, :LANE] = main[:, :LANE].astype(BF16)
        q_out[0, h, :, LANE:] = (main[:, LANE:] * cos + rot * sin + q_code).astype(BF16)
    for c0 in rest:
        project(c0)


def _layer_in(x, mod3, norm_w_row, w_packed, qn_row, kvn_row, wq_all, wrot, w_g2p, b_g2_row, t, *,
              positions=None, tables=None, prev=None):
    b, s, d = x.shape
    n_chunks = s // CHUNK
    assert MLA_ROPE + n_chunks <= LANE, "chunk-mask code does not fit the spare contraction lanes"
    first = prev is None
    assert first == (positions is not None) and first == (tables is None)

    def tok(w):
        return pl.BlockSpec((1, t, w), lambda i, j: (i, j, 0))

    def const(a):
        return pl.BlockSpec(a.shape, lambda i, j: (0,) * a.ndim)

    mod_spec = pl.BlockSpec((1, 3, d), lambda i, j: (i, 0, 0))
    common_specs = [mod_spec, const(norm_w_row), const(w_packed)]
    common_args = [mod3, norm_w_row, w_packed]
    tail_specs = [const(qn_row), const(kvn_row), const(wq_all), const(wrot), const(w_g2p), const(b_g2_row)]
    tail_args = [qn_row, kvn_row, wq_all, wrot, w_g2p, b_g2_row]
    out_specs = [tok(NP_COLS),
                 pl.BlockSpec((1, MLA_HEADS, t, 2 * LANE), lambda i, j: (i, 0, j, 0)),
                 tok(2 * LANE),
                 pl.BlockSpec((1, VT_ROWS, t), lambda i, j: (i, 0, j)),
                 tok(GLA_KWIDTH)]
    out_shape = [jax.ShapeDtypeStruct((b, s, NP_COLS), BF16),
                 jax.ShapeDtypeStruct((b, MLA_HEADS, s, 2 * LANE), BF16),
                 jax.ShapeDtypeStruct((b, s, 2 * LANE), BF16),
                 jax.ShapeDtypeStruct((b, VT_ROWS, s), BF16),
                 jax.ShapeDtypeStruct((b, s, GLA_KWIDTH), F32)]
    if first:
        inv_row = _rope_inv_row()
        in_specs = [tok(d), tok(1), const(inv_row)] + common_specs + tail_specs
        args = [x, positions.reshape(b, s, 1), inv_row] + common_args + tail_args
        out_specs = out_specs + [tok(LANE)] * 4
        out_shape = out_shape + [jax.ShapeDtypeStruct((b, s, LANE), F32)] * 4
    else:
        r_o, m_o, g_o, mod3_prev, w_out_bf = prev
        in_specs = ([tok(RET_WIDTH), tok(MLA_WIDTH), tok(GLA_WIDTH), tok(d), mod_spec, const(w_out_bf)]
                    + common_specs + [tok(LANE)] * len(tables) + tail_specs)
        args = [r_o, m_o, g_o, x, mod3_prev, w_out_bf] + common_args + list(tables) + tail_args
        out_specs = [tok(d)] + out_specs
        out_shape = [jax.ShapeDtypeStruct((b, s, d), F32)] + out_shape
    outs = pl.pallas_call(
        functools.partial(_layer_in_kernel, prev=not first, n_chunks=n_chunks),
        grid=(b, s // t),
        in_specs=in_specs,
        out_specs=out_specs,
        out_shape=out_shape,
        compiler_params=_cparams(("arbitrary", "arbitrary")),
        name="layer_in",
    )(*args)
    return [x] + list(outs) if first else list(outs)


def _out_proj_kernel(r_ref, m_ref, g_ref, x_ref, mod_ref, w_ref, fw_ref, o_ref):
    y = _dot(r_ref[0], w_ref[0:RET_WIDTH, :])
    y += _dot(m_ref[0], w_ref[RET_WIDTH:RET_WIDTH + MLA_WIDTH, :])
    y += _dot(g_ref[0], w_ref[RET_WIDTH + MLA_WIDTH:, :])
    gate = mod_ref[0, 2:3, :]
    x = x_ref[0] + gate * y
    o_ref[0] = x * lax.rsqrt(jnp.mean(x * x, axis=-1, keepdims=True) + EPS) * fw_ref[...]


def _out_proj(r_o, m_o, g_o, x, mod3, w_out_bf, final_row, t):
    b, s, d = x.shape

    def tok(w):
        return pl.BlockSpec((1, t, w), lambda i, j: (i, j, 0))

    return pl.pallas_call(
        _out_proj_kernel,
        grid=(b, s // t),
        in_specs=[tok(RET_WIDTH), tok(MLA_WIDTH), tok(GLA_WIDTH), tok(d),
                  pl.BlockSpec((1, 3, d), lambda i, j: (i, 0, 0)),
                  pl.BlockSpec(w_out_bf.shape, lambda i, j: (0, 0)),
                  pl.BlockSpec((1, d), lambda i, j: (0, 0))],
        out_specs=tok(d),
        out_shape=jax.ShapeDtypeStruct((b, s, d), F32),
        compiler_params=_cparams(("arbitrary", "arbitrary")),
        name="out_proj",
    )(r_o, m_o, g_o, x, mod3, w_out_bf, final_row)


def _pack_w_in(w_in_l):
    d = w_in_l.shape[0]
    cuts = np.cumsum([RET_WIDTH] * 4 + [MLA_Q_RANK, MLA_KV_RANK, MLA_ROPE, MLA_WIDTH]
                     + [GLA_KWIDTH, GLA_KWIDTH, GLA_WIDTH, GLA_GATE_RANK, GLA_WIDTH])[:-1]
    rq, rk, rv, rz, mq, kv, kr, mz, gq, gk, gv, gg, gz = jnp.split(w_in_l, [int(c) for c in cuts], axis=1)

    def halves_first(w):
        w4 = w.reshape(d, RET_HEADS, 2, RET_HEAD_DIM // 2)
        return jnp.swapaxes(w4, 1, 2).reshape(d, RET_WIDTH)

    hm = MLA_ROPE // 2
    kr_rot = jnp.concatenate([-kr[:, hm:], kr[:, :hm]], axis=1)
    mb_pad = jnp.zeros((d, LANE - MB_ROT_LANE - MLA_ROPE), F32)
    assert MB_ROT_LANE == MLA_ROPE + GLA_GATE_RANK
    packed = {
        OFF_MZ: mz, OFF_RQ: halves_first(rq), OFF_RK: halves_first(rk), OFF_RV: rv, OFF_RZ: rz,
        OFF_MQ: mq, OFF_GV: gv, OFF_GZ: gz, OFF_KV: kv,
        OFF_MB: jnp.concatenate([kr, gg, kr_rot, mb_pad], axis=1), OFF_GQ: gq, OFF_GK: gk,
    }
    pieces, pos = [], 0
    for off in sorted(packed):
        assert off == pos, "packed in-projection layout has a gap or overlap"
        pieces.append(packed[off])
        pos += packed[off].shape[1]
    assert pos == NP_COLS
    return jnp.concatenate(pieces, axis=1).astype(BF16)


def _pack_mla_q(w_uq_l, w_abs_l):
    r = w_uq_l.shape[0]
    hd = MLA_NOPE + MLA_ROPE
    hm = MLA_ROPE // 2
    w3 = w_uq_l.reshape(r, MLA_HEADS, hd)
    pe = w3[:, :, MLA_NOPE:]
    pad = jnp.zeros((r, MLA_HEADS, LANE - MLA_ROPE), F32)
    main = jnp.concatenate([jnp.moveaxis(w_abs_l, 0, 1), pe, pad], axis=-1)
    rot = jnp.concatenate([-pe[:, :, hm:], pe[:, :, :hm]], axis=-1)
    return (main.reshape(r, MLA_HEADS * 2 * LANE).astype(BF16),
            rot.reshape(r, MLA_HEADS * MLA_ROPE).astype(BF16))


def kernel(x, c, positions, norm_w, ada_w, ada_b, w_in, mla_q_norm, w_uq, mla_kv_norm, w_ukv,
           gla_w_g2, gla_b_g2, gla_norm, w_out, final_norm):
    b, s, d = x.shape
    depth = w_in.shape[0]
    t_tok = min(512, s)

    mod = _ada_mod(c, ada_w, ada_b).reshape(depth, b, 3, d)

    kv_hd = MLA_NOPE + MLA_V
    q_hd = MLA_NOPE + MLA_ROPE
    w_ukv4 = w_ukv.reshape(depth, MLA_KV_RANK, MLA_HEADS, kv_hd)
    wk_nope = jnp.moveaxis(w_ukv4[..., :MLA_NOPE], 2, 1)
    wuv = jnp.transpose(w_ukv4[..., MLA_NOPE:], (0, 2, 3, 1)).astype(BF16)
    wq_nope = jnp.moveaxis(
        w_uq.reshape(depth, MLA_Q_RANK, MLA_HEADS, q_hd)[..., :MLA_NOPE], 2, 1)
    w_abs = _absorb_weights(wq_nope, wk_nope)

    prev = None
    for l in range(depth):
        w_packed = _pack_w_in(w_in[l])
        wq_all, wrot = _pack_mla_q(w_uq[l], w_abs[l])
        w_g2p = jnp.zeros((LANE, GLA_KWIDTH), F32).at[MLA_ROPE:MLA_ROPE + GLA_GATE_RANK].set(
            gla_w_g2[l]).astype(BF16)

        front = (x, mod[l], norm_w[l].reshape(1, d), w_packed, mla_q_norm[l].reshape(1, MLA_Q_RANK),
                 mla_kv_norm[l].reshape(1, MLA_KV_RANK), wq_all, wrot, w_g2p,
                 gla_b_g2[l].reshape(1, GLA_KWIDTH), t_tok)
        if prev is None:
            x, proj, q_s, k_c, v_t, log_a, cos_r, sin_r, cos_m, sin_m = _layer_in(
                *front, positions=positions)
        else:
            x, proj, q_s, k_c, v_t, log_a = _layer_in(
                *front, tables=(cos_m, sin_m, cos_r, sin_r), prev=prev)
        r_o, g_o = _linear_mixers(proj, log_a, jnp.tile(gla_norm[l], GLA_HEADS).reshape(1, GLA_WIDTH))
        m_o = _mla_attn(q_s, k_c, v_t, proj, wuv[l])
        prev = (r_o, m_o, g_o, mod[l], w_out[l].astype(BF16))
    r_o, m_o, g_o, mod_last, w_out_bf = prev
    t_out = OUT_TILE if s % OUT_TILE == 0 else t_tok
    return _out_proj(r_o, m_o, g_o, x, mod_last, w_out_bf, final_norm.reshape(1, d), t_out)
```
